```python
import math
import jax, jax.numpy as jnp
from jax import lax
import numpy as np

D_MODEL = 2048
BATCH = 8
SEQ = 2048
DEPTH = 2

CHUNK = 128
NORM_EPS = 1e-5

GMLP_WIDTH = D_MODEL
GMLP_GROUPS = 16
GMLP_GROUP_DIM = GMLP_WIDTH // GMLP_GROUPS
SSD_WIDTH = D_MODEL
SSD_HEAD_DIM = 64
SSD_HEADS = SSD_WIDTH // SSD_HEAD_DIM
SSD_GROUPS = 4
SSD_STATE = 128
SSD_CONV = 4
SSD_BC_DIM = SSD_GROUPS * SSD_STATE
SSD_CONV_DIM = SSD_WIDTH + 2 * SSD_BC_DIM
EVEN_MIX = GMLP_WIDTH + SSD_WIDTH
EVEN_SPLITS = (2 * GMLP_WIDTH,
               3 * GMLP_WIDTH,
               3 * GMLP_WIDTH + SSD_WIDTH,
               3 * GMLP_WIDTH + SSD_WIDTH + SSD_CONV_DIM)
EVEN_IN = EVEN_SPLITS[-1] + SSD_HEADS

DIFF_HEADS = 16
DIFF_HEAD_DIM = 64
DIFF_V_DIM = 2 * DIFF_HEAD_DIM
DIFF_WIDTH = DIFF_HEADS * DIFF_V_DIM
ODD_IN = 4 * DIFF_WIDTH

kernel_name = "hybrid_gmlp_ssd_diffattn_block"


def rms_norm(x, g):
    xf = x.astype(jnp.float32)
    y = xf * lax.rsqrt(jnp.mean(xf * xf, axis=-1, keepdims=True) + NORM_EPS)
    return (y * g.astype(jnp.float32)).astype(x.dtype)


def layer_norm(x, g, b):
    xf = x.astype(jnp.float32)
    mu = jnp.mean(xf, axis=-1, keepdims=True)
    var = jnp.mean(jnp.square(xf - mu), axis=-1, keepdims=True)
    y = (xf - mu) * lax.rsqrt(var + NORM_EPS)
    return (y * g.astype(jnp.float32) + b.astype(jnp.float32)).astype(x.dtype)


def gated_group_rms_norm(y, z, g, n_groups):
    yz = y.astype(jnp.float32) * jax.nn.silu(z.astype(jnp.float32))
    shp = yz.shape
    yz = yz.reshape(shp[:-1] + (n_groups, shp[-1] // n_groups))
    yz = yz * lax.rsqrt(jnp.mean(yz * yz, axis=-1, keepdims=True) + NORM_EPS)
    return (yz.reshape(shp) * g.astype(jnp.float32)).astype(z.dtype)


def causal_depthwise_conv(x, w, bias):
    k = w.shape[0]
    out = lax.conv_general_dilated(
        x, w[:, None, :].astype(x.dtype), window_strides=(1,),
        padding=((k - 1, 0),), dimension_numbers=('NWC', 'WIO', 'NWC'),
        feature_group_count=x.shape[-1])
    return out + bias.astype(x.dtype)


def ssd_chunked_scan(x, dt, a, bmat, cmat):
    b, L, H, P = x.shape
    G, N = bmat.shape[2], bmat.shape[3]
    R = H // G
    nc = L // CHUNK
    xdt = (x.astype(jnp.float32) * dt[..., None]).reshape(b, nc, CHUNK, G, R, P)
    adt = (dt * a).reshape(b, nc, CHUNK, G, R)
    a_cum = jnp.cumsum(adt, axis=2)
    bm = bmat.astype(jnp.float32).reshape(b, nc, CHUNK, G, N)
    cm = cmat.astype(jnp.float32).reshape(b, nc, CHUNK, G, N)
    causal = jnp.tril(jnp.ones((CHUNK, CHUNK), dtype=bool))
    seg = a_cum[:, :, :, None] - a_cum[:, :, None, :]
    decay = jnp.exp(jnp.where(causal[None, None, :, :, None, None], seg, -jnp.inf))
    cb = jnp.einsum('bclgn,bcsgn->bclsg', cm, bm)
    y_diag = jnp.einsum('bclsg,bclsgr,bcsgrp->bclgrp', cb, decay, xdt)
    decay_to_end = jnp.exp(a_cum[:, :, -1:] - a_cum)
    states = jnp.einsum('bclgn,bclgr,bclgrp->bcgrpn', bm, decay_to_end, xdt)
    chunk_decay = jnp.exp(a_cum[:, :, -1])

    def step(carry, inp):
        st, dec = inp
        return carry * dec[..., None, None] + st, carry

    init = jnp.zeros((b, G, R, P, N), jnp.float32)
    _, prev = lax.scan(step, init, (jnp.moveaxis(states, 1, 0), jnp.moveaxis(chunk_decay, 1, 0)))
    prev = jnp.moveaxis(prev, 0, 1)
    y_off = jnp.einsum('bclgn,bcgrpn,bclgr->bclgrp', cm, prev, jnp.exp(a_cum))
    return (y_diag + y_off).reshape(b, L, H, P)


def even_layer(x, norm_g, w_in, gmlp_ln_g, gmlp_ln_b, spatial_w, spatial_b,
               conv_w, conv_b, dt_bias, a_log, d_skip, ssm_norm_g, w_out):
    b, L, _ = x.shape
    h = rms_norm(x, norm_g)
    proj = h @ w_in.astype(h.dtype)
    uv, z_a, z_b, xbc, dt_raw = jnp.split(proj, list(EVEN_SPLITS), axis=-1)

    u, v = jnp.split(jax.nn.gelu(uv), 2, axis=-1)
    v = layer_norm(v, gmlp_ln_g, gmlp_ln_b)
    nc = L // CHUNK
    v = v.reshape(b, nc, CHUNK, GMLP_GROUPS, GMLP_GROUP_DIM)
    causal = jnp.tril(jnp.ones((CHUNK, CHUNK), dtype=bool))
    w_s = jnp.where(causal[None], spatial_w, 0).astype(v.dtype)
    v_mix = jnp.einsum('gts,bnsgc->bntgc', w_s, v) + spatial_b.T.astype(v.dtype)[None, None, :, :, None]
    y_a = u * v_mix.reshape(b, L, GMLP_WIDTH) * jax.nn.silu(z_a)

    xbc = jax.nn.silu(causal_depthwise_conv(xbc, conv_w, conv_b))
    xs, bmat, cmat = jnp.split(xbc, [SSD_WIDTH, SSD_WIDTH + SSD_BC_DIM], axis=-1)
    xs = xs.reshape(b, L, SSD_HEADS, SSD_HEAD_DIM)
    bmat = bmat.reshape(b, L, SSD_GROUPS, SSD_STATE)
    cmat = cmat.reshape(b, L, SSD_GROUPS, SSD_STATE)
    dt = jax.nn.softplus(dt_raw.astype(jnp.float32) + dt_bias.astype(jnp.float32))
    a = -jnp.exp(a_log.astype(jnp.float32))
    y = ssd_chunked_scan(xs, dt, a, bmat, cmat) + d_skip.astype(jnp.float32)[:, None] * xs.astype(jnp.float32)
    y_b = gated_group_rms_norm(y.reshape(b, L, SSD_WIDTH), z_b, ssm_norm_g, SSD_GROUPS)

    y_cat = jnp.concatenate([y_a, y_b.astype(y_a.dtype)], axis=-1)
    return x + (y_cat @ w_out.astype(y_cat.dtype)).astype(x.dtype)


def odd_layer(x, norm_g, w_in, lambda_q1, lambda_k1, lambda_q2, lambda_k2,
              subln_g, w_out, layer_idx):
    b, L, _ = x.shape
    h = rms_norm(x, norm_g)
    proj = h @ w_in.astype(h.dtype)
    q, k, v, gate = jnp.split(proj, 4, axis=-1)
    q = q.reshape(b, L, DIFF_HEADS, 2, DIFF_HEAD_DIM).transpose(0, 2, 3, 1, 4)
    k = k.reshape(b, L, DIFF_HEADS, 2, DIFF_HEAD_DIM).transpose(0, 2, 3, 1, 4)
    v = v.reshape(b, L, DIFF_HEADS, DIFF_V_DIM).transpose(0, 2, 1, 3)

    lambda_init = 0.8 - 0.6 * math.exp(-0.3 * layer_idx)
    lam = (jnp.exp(jnp.sum(lambda_q1.astype(jnp.float32) * lambda_k1.astype(jnp.float32)))
           - jnp.exp(jnp.sum(lambda_q2.astype(jnp.float32) * lambda_k2.astype(jnp.float32)))
           + lambda_init)
    slopes = 2.0 ** (-8.0 * (jnp.arange(DIFF_HEADS, dtype=jnp.float32) + 1.0) / DIFF_HEADS)
    scale = DIFF_HEAD_DIM ** -0.5

    outs = []
    for i in range(L // CHUNK):
        n_k = (i + 1) * CHUNK
        q_blk = q[:, :, :, i * CHUNK:n_k]
        s = jnp.einsum('bhiqd,bhikd->bhiqk', q_blk, k[:, :, :, :n_k]).astype(jnp.float32) * scale
        q_pos = i * CHUNK + jnp.arange(CHUNK)
        dist = (q_pos[:, None] - jnp.arange(n_k)[None, :]).astype(jnp.float32)
        s = s - slopes[:, None, None, None] * dist
        s = jnp.where(dist >= 0, s, -jnp.inf)
        p = jax.nn.softmax(s, axis=-1)
        attn = p[:, :, 0] - lam * p[:, :, 1]
        outs.append(jnp.einsum('bhqk,bhkv->bhqv', attn.astype(v.dtype), v[:, :, :n_k]))
    o = jnp.concatenate(outs, axis=2)
    o = rms_norm(o, subln_g) * (1.0 - lambda_init)
    o = o.transpose(0, 2, 1, 3).reshape(b, L, DIFF_WIDTH) * jax.nn.silu(gate)
    return x + (o @ w_out.astype(o.dtype)).astype(x.dtype)


def setup_inputs(seed: int = 0) -> dict:
    key = jax.random.key(seed)
    ks = jax.random.split(key, 24)
    f32 = jnp.float32
    nrm = lambda k, shp, s: jax.random.normal(k, shp, f32) * s
    x = jax.random.normal(ks[0], (BATCH, SEQ, D_MODEL), f32)
    causal = jnp.tril(jnp.ones((CHUNK, CHUNK), f32))
    dt = jnp.exp(jax.random.uniform(ks[9], (SSD_HEADS,), f32) * (math.log(0.1) - math.log(0.001)) + math.log(0.001))
    return {
        "x": x,
        "l0_norm_g": 1.0 + nrm(ks[1], (D_MODEL,), 0.02),
        "l0_w_in": nrm(ks[2], (D_MODEL, EVEN_IN), D_MODEL ** -0.5),
        "l0_gmlp_ln_g": 1.0 + nrm(ks[3], (GMLP_WIDTH,), 0.02),
        "l0_gmlp_ln_b": nrm(ks[4], (GMLP_WIDTH,), 0.02),
        "l0_spatial_w": nrm(ks[5], (GMLP_GROUPS, CHUNK, CHUNK), 1.0) * causal * lax.rsqrt(jnp.arange(1, CHUNK + 1, dtype=f32))[:, None],
        "l0_spatial_b": 1.0 + nrm(ks[6], (GMLP_GROUPS, CHUNK), 0.1),
        "l0_conv_w": nrm(ks[7], (SSD_CONV, SSD_CONV_DIM), SSD_CONV ** -0.5),
        "l0_conv_b": nrm(ks[8], (SSD_CONV_DIM,), 0.02),
        "l0_dt_bias": dt + jnp.log(-jnp.expm1(-dt)),
        "l0_a_log": jnp.log(jax.random.uniform(ks[10], (SSD_HEADS,), f32, 1.0, 16.0)),
        "l0_d_skip": 1.0 + nrm(ks[11], (SSD_HEADS,), 0.1),
        "l0_ssm_norm_g": 1.0 + nrm(ks[12], (SSD_WIDTH,), 0.02),
        "l0_w_out": nrm(ks[13], (EVEN_MIX, D_MODEL), EVEN_MIX ** -0.5),
        "l1_norm_g": 1.0 + nrm(ks[14], (D_MODEL,), 0.02),
        "l1_w_in": nrm(ks[15], (D_MODEL, ODD_IN), D_MODEL ** -0.5),
        "l1_lambda_q1": nrm(ks[16], (DIFF_HEAD_DIM,), 0.1),
        "l1_lambda_k1": nrm(ks[17], (DIFF_HEAD_DIM,), 0.1),
        "l1_lambda_q2": nrm(ks[18], (DIFF_HEAD_DIM,), 0.1),
        "l1_lambda_k2": nrm(ks[19], (DIFF_HEAD_DIM,), 0.1),
        "l1_subln_g": 1.0 + nrm(ks[20], (DIFF_V_DIM,), 0.02),
        "l1_w_out": nrm(ks[21], (DIFF_WIDTH, D_MODEL), DIFF_WIDTH ** -0.5),
        "final_norm_g": 1.0 + nrm(ks[22], (D_MODEL,), 0.02),
    }


def reference(x, l0_norm_g, l0_w_in, l0_gmlp_ln_g, l0_gmlp_ln_b, l0_spatial_w,
              l0_spatial_b, l0_conv_w, l0_conv_b, l0_dt_bias, l0_a_log, l0_d_skip,
              l0_ssm_norm_g, l0_w_out, l1_norm_g, l1_w_in, l1_lambda_q1,
              l1_lambda_k1, l1_lambda_q2, l1_lambda_k2, l1_subln_g, l1_w_out,
              final_norm_g):
    for layer in range(DEPTH):
        if layer % 2 == 0:
            x = even_layer(x, l0_norm_g, l0_w_in, l0_gmlp_ln_g, l0_gmlp_ln_b,
                           l0_spatial_w, l0_spatial_b, l0_conv_w, l0_conv_b,
                           l0_dt_bias, l0_a_log, l0_d_skip, l0_ssm_norm_g, l0_w_out)
        else:
            x = odd_layer(x, l1_norm_g, l1_w_in, l1_lambda_q1, l1_lambda_k1,
                          l1_lambda_q2, l1_lambda_k2, l1_subln_g, l1_w_out, layer)
    return rms_norm(x, final_norm_g)
```

```python
import functools
import math

import jax
import jax.numpy as jnp
from jax import lax
from jax.experimental import pallas as pl
from jax.experimental.pallas import tpu as pltpu

F32 = jnp.float32
BF16 = jnp.bfloat16

D_MODEL = 2048
SEQ = 2048
CHUNK = 128
NORM_EPS = 1e-5

GMLP_WIDTH = 2048
GMLP_GROUPS = 16
SSD_WIDTH = 2048
SSD_HEADS = 32
SSD_HEAD_DIM = 64
SSD_GROUPS = 4
SSD_STATE = 128
SSD_CONV = 4
SSD_BC_DIM = SSD_GROUPS * SSD_STATE
SSD_CONV_DIM = SSD_WIDTH + 2 * SSD_BC_DIM
SSD_GROUP_WIDTH = SSD_WIDTH // SSD_GROUPS
UVZ_WIDTH = 3 * GMLP_WIDTH + SSD_WIDTH
EVEN_MAIN = UVZ_WIDTH + SSD_CONV_DIM

DIFF_HEADS = 16
DIFF_HEAD_DIM = 64
DIFF_V_DIM = 128
DIFF_WIDTH = DIFF_HEADS * DIFF_V_DIM
LAMBDA_INIT = 0.8 - 0.6 * math.exp(-0.3 * 1)

LANES = 128
HALO = 8
VMEM_LIMIT = 56 * 1024 * 1024


def _sigmoid(x):
    return 1.0 / (1.0 + jnp.exp(-x))


def _silu(x):
    return x * _sigmoid(x)


def _gelu_tanh(x):
    c = math.sqrt(2.0 / math.pi)
    return 0.5 * x * (1.0 + jnp.tanh(c * (x + 0.044715 * (x * x * x))))


def _softplus(x):
    return jnp.maximum(x, 0.0) + jnp.log1p(jnp.exp(-jnp.abs(x)))


def _norm_matmul_kernel(*refs, bounds, has_small):
    x_ref, g_ref, w_ref = refs[:3]
    pos = 3
    ws_ref = None
    if has_small:
        ws_ref = refs[pos]
        pos += 1
    out_refs = refs[pos:pos + len(bounds)]
    pos += len(bounds)
    small_ref = None
    if has_small:
        small_ref = refs[pos]
        pos += 1
    h_ref = refs[pos]

    j = pl.program_id(1)

    @pl.when(j == 0)
    def _():
        xf = x_ref[...]
        ms = jnp.mean(xf * xf, axis=-1, keepdims=True)
        h = (xf * lax.rsqrt(ms + NORM_EPS) * g_ref[...]).astype(BF16)
        h_ref[...] = h
        if has_small:
            small_ref[...] = jnp.dot(h, ws_ref[...], preferred_element_type=F32)

    acc = jnp.dot(h_ref[...], w_ref[...], preferred_element_type=F32)
    if len(bounds) == 1:
        out_refs[0][...] = acc.astype(out_refs[0].dtype)
    else:
        for (lo, hi), o_ref in zip(bounds, out_refs):
            @pl.when((j >= lo) & (j < hi))
            def _(o_ref=o_ref):
                o_ref[...] = acc.astype(o_ref.dtype)


def _norm_matmul(x, g, w, segs, *, tm, tn, w_small=None):
    m, k = x.shape
    n = w.shape[1]
    assert m % tm == 0 and n % tn == 0
    bounds = []
    lo = 0
    for ncols, _ in segs:
        assert ncols % tn == 0
        bounds.append((lo // tn, (lo + ncols) // tn))
        lo += ncols
    assert lo == n

    in_specs = [
        pl.BlockSpec((tm, k), lambda i, j: (i, 0)),
        pl.BlockSpec((1, k), lambda i, j: (0, 0)),
        pl.BlockSpec((k, tn), lambda i, j: (0, j)),
    ]
    args = [x, g.reshape(1, k), w]
    if w_small is not None:
        in_specs.append(pl.BlockSpec((k, w_small.shape[1]), lambda i, j: (0, 0)))
        args.append(w_small)

    out_specs = []
    out_shape = []
    for (blo, bhi), (ncols, dtype) in zip(bounds, segs):
        def idx(i, j, blo=blo, bhi=bhi):
            return (i, jnp.clip(j - blo, 0, bhi - blo - 1))
        out_specs.append(pl.BlockSpec((tm, tn), idx))
        out_shape.append(jax.ShapeDtypeStruct((m, ncols), dtype))
    if w_small is not None:
        ns = w_small.shape[1]
        out_specs.append(pl.BlockSpec((tm, ns), lambda i, j: (i, 0)))
        out_shape.append(jax.ShapeDtypeStruct((m, ns), F32))

    return pl.pallas_call(
        functools.partial(_norm_matmul_kernel, bounds=tuple(bounds),
                          has_small=w_small is not None),
        grid=(m // tm, n // tn),
        in_specs=in_specs,
        out_specs=out_specs,
        out_shape=out_shape,
        scratch_shapes=[pltpu.VMEM((tm, k), BF16)],
        compiler_params=pltpu.CompilerParams(
            dimension_semantics=("arbitrary", "arbitrary"),
            vmem_limit_bytes=VMEM_LIMIT),
        name="norm_matmul",
    )(*args)


def _gmlp_kernel(u_ref, v_ref, z_ref, lng_ref, lnb_ref, ws_ref, sb_ref, o_ref,
                 vn_ref, *, rows):
    v = _gelu_tanh(v_ref[...])
    mu = jnp.mean(v, axis=-1, keepdims=True)
    vc = v - mu
    var = jnp.mean(vc * vc, axis=-1, keepdims=True)
    vn = vc * lax.rsqrt(var + NORM_EPS) * lng_ref[...] + lnb_ref[...]
    vn_ref[...] = vn.astype(BF16)

    row = lax.broadcasted_iota(jnp.int32, (CHUNK, CHUNK), 0)
    col = lax.broadcasted_iota(jnp.int32, (CHUNK, CHUNK), 1)
    causal = row >= col
    for g in range(GMLP_GROUPS):
        cs = slice(g * LANES, (g + 1) * LANES)
        w_s = jnp.where(causal, ws_ref[g], 0.0).astype(BF16)
        for c in range(rows // CHUNK):
            rs = slice(c * CHUNK, (c + 1) * CHUNK)
            v_mix = jnp.dot(w_s, vn_ref[rs, cs], preferred_element_type=F32)
            v_mix = v_mix + sb_ref[:, cs]
            u = _gelu_tanh(u_ref[rs, cs])
            o_ref[rs, cs] = (u * v_mix * _silu(z_ref[rs, cs])).astype(o_ref.dtype)


def _gmlp(uvz, ln_g, ln_b, spatial_w, spatial_b_exp, *, rows):
    m = uvz.shape[0]
    w = GMLP_WIDTH
    return pl.pallas_call(
        functools.partial(_gmlp_kernel, rows=rows),
        grid=(m // rows,),
        in_specs=[
            pl.BlockSpec((rows, w), lambda i: (i, 0)),
            pl.BlockSpec((rows, w), lambda i: (i, 1)),
            pl.BlockSpec((rows, w), lambda i: (i, 2)),
            pl.BlockSpec((1, w), lambda i: (0, 0)),
            pl.BlockSpec((1, w), lambda i: (0, 0)),
            pl.BlockSpec((GMLP_GROUPS, CHUNK, CHUNK), lambda i: (0, 0, 0)),
            pl.BlockSpec((CHUNK, w), lambda i: (0, 0)),
        ],
        out_specs=pl.BlockSpec((rows, w), lambda i: (i, 0)),
        out_shape=jax.ShapeDtypeStruct((m, w), BF16),
        scratch_shapes=[pltpu.VMEM((rows, w), BF16)],
        compiler_params=pltpu.CompilerParams(
            dimension_semantics=("arbitrary",), vmem_limit_bytes=VMEM_LIMIT),
        name="gmlp",
    )(uvz, uvz, uvz, ln_g.reshape(1, w), ln_b.reshape(1, w), spatial_w,
      spatial_b_exp)


def _ssd_kernel(zb_ref, xbc_ref, dt_ref, cw_ref, cb_ref, dtb_ref, alog_ref,
                dskip_ref, ng_ref, e_ref, o_ref, ext_ref, st_ref, y_ref):
    c = pl.program_id(1)

    @pl.when(c == 0)
    def _():
        ext_ref[0:HALO, :] = jnp.zeros((HALO, SSD_CONV_DIM), F32)
        st_ref[...] = jnp.zeros(st_ref.shape, F32)

    @pl.when(c != 0)
    def _():
        ext_ref[0:HALO, :] = ext_ref[CHUNK:CHUNK + HALO, :]

    ext_ref[HALO:HALO + CHUNK, :] = xbc_ref[...]

    conv = cb_ref[...] + cw_ref[0:1, :] * ext_ref[HALO - 3:HALO - 3 + CHUNK, :]
    for k in range(1, SSD_CONV):
        off = HALO - (SSD_CONV - 1) + k
        conv = conv + cw_ref[k:k + 1, :] * ext_ref[off:off + CHUNK, :]
    xact = _silu(conv)
    xs = xact[:, :SSD_WIDTH]

    dt = _softplus(dt_ref[...] + dtb_ref[...])
    adt = dt * (-jnp.exp(alog_ref[...]))
    row = lax.broadcasted_iota(jnp.int32, (CHUNK, CHUNK), 0)
    col = lax.broadcasted_iota(jnp.int32, (CHUNK, CHUNK), 1)
    causal = row >= col
    tri = causal.astype(F32)
    a_cum = jnp.dot(tri, adt, precision=lax.Precision.HIGHEST,
                    preferred_element_type=F32)
    a_cum_t = a_cum.T

    expanded = jnp.dot(jnp.concatenate([dt, a_cum], axis=0), e_ref[...],
                       precision=lax.Precision.HIGHEST,
                       preferred_element_type=F32)
    dt_exp = expanded[:CHUNK]
    acum_exp = expanded[CHUNK:]
    alast_exp = acum_exp[CHUNK - 1:CHUNK, :]
    exp_a = jnp.exp(acum_exp)
    decay_to_end = jnp.exp(alast_exp - acum_exp)
    chunk_decay = jnp.exp(alast_exp)

    xdt = xs * dt_exp
    lane = lax.broadcasted_iota(jnp.int32, (CHUNK, SSD_WIDTH), 1)
    low_half = (lane % LANES) < SSD_HEAD_DIM
    xdt_lo = jnp.where(low_half, xdt, 0.0).astype(BF16)
    xdt_hi = jnp.where(low_half, 0.0, xdt).astype(BF16)
    xdte = (xdt * decay_to_end).astype(BF16)

    heads_per_group = SSD_HEADS // SSD_GROUPS
    for g in range(SSD_GROUPS):
        gs = slice(g * SSD_GROUP_WIDTH, (g + 1) * SSD_GROUP_WIDTH)
        b_g = xact[:, SSD_WIDTH + g * SSD_STATE:SSD_WIDTH + (g + 1) * SSD_STATE]
        c_g = xact[:, SSD_WIDTH + SSD_BC_DIM + g * SSD_STATE:
                   SSD_WIDTH + SSD_BC_DIM + (g + 1) * SSD_STATE].astype(BF16)
        b_gt = b_g.T.astype(BF16)
        cb = jnp.dot(c_g, b_gt, preferred_element_type=F32)

        st_old = st_ref[g]
        y_off = jnp.dot(c_g, st_old.astype(BF16), preferred_element_type=F32)
        y_off = y_off * exp_a[:, gs]

        for jp in range(heads_per_group // 2):
            blk = g * (heads_per_group // 2) + jp
            bs = slice(blk * LANES, (blk + 1) * LANES)
            ms = []
            for hh in range(2):
                h = 2 * blk + hh
                seg = a_cum[:, h:h + 1] - a_cum_t[h:h + 1, :]
                decay = jnp.exp(jnp.where(causal, seg, -jnp.inf))
                ms.append((cb * decay).astype(BF16))
            lhs = jnp.concatenate(ms, axis=1)
            rhs = jnp.concatenate([xdt_lo[:, bs], xdt_hi[:, bs]], axis=0)
            y_diag = jnp.dot(lhs, rhs, preferred_element_type=F32)
            y_ref[:, bs] = (y_diag + y_off[:, jp * LANES:(jp + 1) * LANES]
                            + dskip_ref[:, bs] * xs[:, bs])

        st_ref[g] = (st_old * chunk_decay[:, gs]
                     + jnp.dot(b_gt, xdte[:, gs], preferred_element_type=F32))

    for g in range(SSD_GROUPS):
        gs = slice(g * SSD_GROUP_WIDTH, (g + 1) * SSD_GROUP_WIDTH)
        yz = y_ref[:, gs] * _silu(zb_ref[:, gs])
        ms = jnp.mean(yz * yz, axis=-1, keepdims=True)
        o_ref[:, gs] = (yz * lax.rsqrt(ms + NORM_EPS) * ng_ref[:, gs]).astype(o_ref.dtype)


def _ssd(uvz, xbc, dt_raw, conv_w, conv_b, dt_bias_p, a_log_p, d_skip_exp,
         norm_g, expand, *, batch):
    m = uvz.shape[0]
    nc = m // batch // CHUNK
    full = lambda shape: pl.BlockSpec(shape, lambda b, c: (0,) * len(shape))
    return pl.pallas_call(
        _ssd_kernel,
        grid=(batch, nc),
        in_specs=[
            pl.BlockSpec((CHUNK, SSD_WIDTH), lambda b, c: (b * nc + c, 3)),
            pl.BlockSpec((CHUNK, SSD_CONV_DIM), lambda b, c: (b * nc + c, 0)),
            pl.BlockSpec((CHUNK, LANES), lambda b, c: (b * nc + c, 0)),
            full((SSD_CONV, SSD_CONV_DIM)),
            full((1, SSD_CONV_DIM)),
            full((1, LANES)),
            full((1, LANES)),
            full((1, SSD_WIDTH)),
            full((1, SSD_WIDTH)),
            full((LANES, SSD_WIDTH)),
        ],
        out_specs=pl.BlockSpec((CHUNK, SSD_WIDTH), lambda b, c: (b * nc + c, 0)),
        out_shape=jax.ShapeDtypeStruct((m, SSD_WIDTH), BF16),
        scratch_shapes=[
            pltpu.VMEM((CHUNK + HALO, SSD_CONV_DIM), F32),
            pltpu.VMEM((SSD_GROUPS, SSD_STATE, SSD_GROUP_WIDTH), F32),
            pltpu.VMEM((CHUNK, SSD_WIDTH), F32),
        ],
        compiler_params=pltpu.CompilerParams(
            dimension_semantics=("arbitrary", "arbitrary"),
            vmem_limit_bytes=VMEM_LIMIT),
        name="ssd",
    )(uvz, xbc, dt_raw, conv_w, conv_b.reshape(1, -1), dt_bias_p, a_log_p,
      d_skip_exp, norm_g.reshape(1, -1), expand)


def _out_proj2_kernel(ya_ref, yb_ref, wa_ref, wb_ref, x_ref, o_ref):
    acc = jnp.dot(ya_ref[...], wa_ref[...], preferred_element_type=F32)
    acc = acc + jnp.dot(yb_ref[...], wb_ref[...], preferred_element_type=F32)
    o_ref[...] = x_ref[...] + acc


def _out_proj2(ya, yb, wa, wb, x, *, tm, tn):
    m, k = ya.shape
    n = wa.shape[1]
    return pl.pallas_call(
        _out_proj2_kernel,
        grid=(m // tm, n // tn),
        in_specs=[
            pl.BlockSpec((tm, k), lambda i, j: (i, 0)),
            pl.BlockSpec((tm, k), lambda i, j: (i, 0)),
            pl.BlockSpec((k, tn), lambda i, j: (0, j)),
            pl.BlockSpec((k, tn), lambda i, j: (0, j)),
            pl.BlockSpec((tm, tn), lambda i, j: (i, j)),
        ],
        out_specs=pl.BlockSpec((tm, tn), lambda i, j: (i, j)),
        out_shape=jax.ShapeDtypeStruct((m, n), F32),
        compiler_params=pltpu.CompilerParams(
            dimension_semantics=("arbitrary", "arbitrary"),
            vmem_limit_bytes=VMEM_LIMIT),
        name="out_proj_even",
    )(ya, yb, wa, wb, x)


def _out_proj_norm_kernel(y_ref, w_ref, x_ref, g_ref, o_ref):
    x2 = x_ref[...] + jnp.dot(y_ref[...], w_ref[...], preferred_element_type=F32)
    ms = jnp.mean(x2 * x2, axis=-1, keepdims=True)
    o_ref[...] = x2 * lax.rsqrt(ms + NORM_EPS) * g_ref[...]


def _out_proj_norm(y, w, x, g, *, tm):
    m, k = y.shape
    n = w.shape[1]
    return pl.pallas_call(
        _out_proj_norm_kernel,
        grid=(m // tm,),
        in_specs=[
            pl.BlockSpec((tm, k), lambda i: (i, 0)),
            pl.BlockSpec((k, n), lambda i: (0, 0)),
            pl.BlockSpec((tm, n), lambda i: (i, 0)),
            pl.BlockSpec((1, n), lambda i: (0, 0)),
        ],
        out_specs=pl.BlockSpec((tm, n), lambda i: (i, 0)),
        out_shape=jax.ShapeDtypeStruct((m, n), F32),
        compiler_params=pltpu.CompilerParams(
            dimension_semantics=("arbitrary",), vmem_limit_bytes=VMEM_LIMIT),
        name="out_proj_odd_norm",
    )(y, w, x, g.reshape(1, n))


def _diff_attn_kernel(slopes_ref, q_ref, k_ref, v_ref, gate_ref, lq1_ref, lk1_ref,
                      lq2_ref, lk2_ref, sg_ref, o_ref, *, tq):
    h = pl.program_id(1)
    qi = pl.program_id(2)
    slope = slopes_ref[h]
    q = q_ref[...] * jnp.asarray(DIFF_HEAD_DIM ** -0.5, BF16)

    lane = lax.broadcasted_iota(jnp.int32, (tq, LANES), 1)
    first_map = lane < DIFF_HEAD_DIM
    kpos0 = lax.broadcasted_iota(jnp.int32, (1, tq), 1)
    nt_dims = (((1,), (1,)), ((), ()))

    def scores(j):
        start = pl.multiple_of(j * tq, tq)
        k = k_ref[pl.ds(start, tq), :]
        v = v_ref[pl.ds(start, tq), :]
        k1 = jnp.where(first_map, k, jnp.zeros_like(k))
        k2 = jnp.where(first_map, jnp.zeros_like(k), k)
        bias = slope * (kpos0 + (j - qi) * tq).astype(F32)
        s1 = lax.dot_general(q, k1, nt_dims, preferred_element_type=F32) + bias
        s2 = lax.dot_general(q, k2, nt_dims, preferred_element_type=F32) + bias
        return s1, s2, v

    def update(s, m, l, acc, v):
        m_new = jnp.maximum(m, jnp.max(s, axis=-1, keepdims=True))
        alpha = jnp.exp(m - m_new)
        p = jnp.exp(s - m_new)
        l = alpha * l + jnp.sum(p, axis=-1, keepdims=True)
        acc = alpha * acc + jnp.dot(p.astype(BF16), v, preferred_element_type=F32)
        return m_new, l, acc

    def body(j, carry):
        m1, l1, a1, m2, l2, a2 = carry
        s1, s2, v = scores(j)
        m1, l1, a1 = update(s1, m1, l1, a1, v)
        m2, l2, a2 = update(s2, m2, l2, a2, v)
        return m1, l1, a1, m2, l2, a2

    m0 = jnp.full((tq, 1), -1e30, F32)
    l0 = jnp.zeros((tq, 1), F32)
    a0 = jnp.zeros((tq, DIFF_V_DIM), F32)
    carry = lax.fori_loop(0, qi, body, (m0, l0, a0, m0, l0, a0))

    m1, l1, a1, m2, l2, a2 = carry
    s1, s2, v = scores(qi)
    row = lax.broadcasted_iota(jnp.int32, (tq, tq), 0)
    col = lax.broadcasted_iota(jnp.int32, (tq, tq), 1)
    keep = row >= col
    s1 = jnp.where(keep, s1, -jnp.inf)
    s2 = jnp.where(keep, s2, -jnp.inf)
    m1, l1, a1 = update(s1, m1, l1, a1, v)
    m2, l2, a2 = update(s2, m2, l2, a2, v)

    lam = (jnp.exp(jnp.sum(lq1_ref[...] * lk1_ref[...], axis=-1, keepdims=True))
           - jnp.exp(jnp.sum(lq2_ref[...] * lk2_ref[...], axis=-1, keepdims=True))
           + LAMBDA_INIT)
    o = a1 / l1 - lam * (a2 / l2)
    ms = jnp.mean(o * o, axis=-1, keepdims=True)
    o = o * lax.rsqrt(ms + NORM_EPS) * sg_ref[...]
    o = o * (1.0 - LAMBDA_INIT)
    o_ref[...] = (o * _silu(gate_ref[...])).astype(o_ref.dtype)


def _diff_attn(qkv, gate, slopes, lq1, lk1, lq2, lk2, subln_g, *, batch, tq):
    m = qkv.shape[0]
    nq = SEQ // tq
    vec = lambda n: pl.BlockSpec((1, n), lambda b, h, i, s: (0, 0))
    grid_spec = pltpu.PrefetchScalarGridSpec(
        num_scalar_prefetch=1,
        grid=(batch, DIFF_HEADS, nq),
        in_specs=[
            pl.BlockSpec((tq, LANES), lambda b, h, i, s: (b * nq + i, h)),
            pl.BlockSpec((SEQ, LANES), lambda b, h, i, s: (b, DIFF_HEADS + h)),
            pl.BlockSpec((SEQ, LANES), lambda b, h, i, s: (b, 2 * DIFF_HEADS + h)),
            pl.BlockSpec((tq, LANES), lambda b, h, i, s: (b * nq + i, h)),
            vec(DIFF_HEAD_DIM), vec(DIFF_HEAD_DIM), vec(DIFF_HEAD_DIM),
            vec(DIFF_HEAD_DIM), vec(DIFF_V_DIM),
        ],
        out_specs=pl.BlockSpec((tq, LANES), lambda b, h, i, s: (b * nq + i, h)),
    )
    return pl.pallas_call(
        functools.partial(_diff_attn_kernel, tq=tq),
        grid_spec=grid_spec,
        out_shape=jax.ShapeDtypeStruct((m, DIFF_WIDTH), BF16),
        compiler_params=pltpu.CompilerParams(
            dimension_semantics=("arbitrary", "arbitrary", "arbitrary"),
            vmem_limit_bytes=VMEM_LIMIT),
        name="diff_attn",
    )(slopes, qkv, qkv, qkv, gate, lq1.reshape(1, -1), lk1.reshape(1, -1),
      lq2.reshape(1, -1), lk2.reshape(1, -1), subln_g.reshape(1, -1))


def kernel(x, l0_norm_g, l0_w_in, l0_gmlp_ln_g, l0_gmlp_ln_b, l0_spatial_w,
           l0_spatial_b, l0_conv_w, l0_conv_b, l0_dt_bias, l0_a_log, l0_d_skip,
           l0_ssm_norm_g, l0_w_out, l1_norm_g, l1_w_in, l1_lambda_q1,
           l1_lambda_k1, l1_lambda_q2, l1_lambda_k2, l1_subln_g, l1_w_out,
           final_norm_g):
    batch, seq, d = x.shape
    m = batch * seq
    x2d = x.reshape(m, d)

    w_main = l0_w_in[:, :EVEN_MAIN].astype(BF16)
    w_dt = jnp.pad(l0_w_in[:, EVEN_MAIN:], ((0, 0), (0, LANES - SSD_HEADS))).astype(BF16)
    uvz, xbc, dt_raw = _norm_matmul(
        x2d, l0_norm_g, w_main, [(UVZ_WIDTH, F32), (SSD_CONV_DIM, F32)],
        tm=1024, tn=512, w_small=w_dt)

    spatial_b_exp = jnp.repeat(l0_spatial_b.T, LANES, axis=1)
    y_a = _gmlp(uvz, l0_gmlp_ln_g, l0_gmlp_ln_b, l0_spatial_w, spatial_b_exp,
                rows=256)

    pad_h = LANES - SSD_HEADS
    dt_bias_p = jnp.pad(l0_dt_bias, (0, pad_h)).reshape(1, LANES)
    a_log_p = jnp.pad(l0_a_log, (0, pad_h)).reshape(1, LANES)
    d_skip_exp = jnp.repeat(l0_d_skip, SSD_HEAD_DIM).reshape(1, SSD_WIDTH)
    expand = (jnp.arange(LANES)[:, None]
              == (jnp.arange(SSD_WIDTH) // SSD_HEAD_DIM)[None, :]).astype(F32)
    y_b = _ssd(uvz, xbc, dt_raw, l0_conv_w, l0_conv_b, dt_bias_p, a_log_p,
               d_skip_exp, l0_ssm_norm_g, expand, batch=batch)

    w_out0 = l0_w_out.astype(BF16)
    x1 = _out_proj2(y_a, y_b, w_out0[:GMLP_WIDTH], w_out0[GMLP_WIDTH:], x2d,
                    tm=1024, tn=512)

    qkv, gate = _norm_matmul(
        x1, l1_norm_g, l1_w_in.astype(BF16),
        [(3 * DIFF_WIDTH, BF16), (DIFF_WIDTH, F32)], tm=1024, tn=512)
    slopes = 2.0 ** (-8.0 * (jnp.arange(DIFF_HEADS, dtype=F32) + 1.0) / DIFF_HEADS)
    o = _diff_attn(qkv, gate, slopes, l1_lambda_q1, l1_lambda_k1, l1_lambda_q2,
                   l1_lambda_k2, l1_subln_g, batch=batch, tq=512)

    out = _out_proj_norm(o, l1_w_out.astype(BF16), x1, final_norm_g, tm=512)
    return out.reshape(batch, seq, d)
```

```python
import functools
import math

import jax
import jax.numpy as jnp
from jax import lax
from jax.experimental import pallas as pl
from jax.experimental.pallas import tpu as pltpu

F32 = jnp.float32
BF16 = jnp.bfloat16

D_MODEL = 2048
SEQ = 2048
CHUNK = 128
NORM_EPS = 1e-5

GMLP_WIDTH = 2048
GMLP_GROUPS = 16
SSD_WIDTH = 2048
SSD_HEADS = 32
SSD_HEAD_DIM = 64
SSD_GROUPS = 4
SSD_STATE = 128
SSD_CONV = 4
SSD_BC_DIM = SSD_GROUPS * SSD_STATE
SSD_CONV_DIM = SSD_WIDTH + 2 * SSD_BC_DIM
SSD_GROUP_WIDTH = SSD_WIDTH // SSD_GROUPS
UVZ_WIDTH = 3 * GMLP_WIDTH + SSD_WIDTH
EVEN_MAIN = UVZ_WIDTH + SSD_CONV_DIM

DIFF_HEADS = 16
DIFF_HEAD_DIM = 64
DIFF_V_DIM = 128
DIFF_WIDTH = DIFF_HEADS * DIFF_V_DIM
LAMBDA_INIT = 0.8 - 0.6 * math.exp(-0.3 * 1)
LOG2E = math.log2(math.e)

LANES = 128
HALO = 8
VMEM_LIMIT = 56 * 1024 * 1024


def _silu(x):
    half = 0.5 * x
    return half + half * jnp.tanh(half)


SPLIT_TERMS = 3


def _split_f32(x):
    t1 = x.astype(BF16).astype(F32)
    r1 = x - t1
    t2 = r1.astype(BF16).astype(F32)
    t3 = (r1 - t2).astype(BF16).astype(F32)
    return t1, t2, t3


def _gelu_tanh(x):
    c = math.sqrt(2.0 / math.pi)
    return 0.5 * x * (1.0 + jnp.tanh(c * (x + 0.044715 * (x * x * x))))


def _softplus(x):
    return jnp.maximum(x, 0.0) + jnp.log1p(jnp.exp(-jnp.abs(x)))


def _norm_matmul_kernel(*refs, bounds, scales, has_small):
    x_ref, g_ref, w_ref = refs[:3]
    pos = 3
    ws_ref = None
    if has_small:
        ws_ref = refs[pos]
        pos += 1
    out_refs = refs[pos:pos + len(bounds)]
    pos += len(bounds)
    small_ref = None
    if has_small:
        small_ref = refs[pos]
        pos += 1
    h_ref = refs[pos]

    j = pl.program_id(1)

    @pl.when(j == 0)
    def _():
        xf = x_ref[...]
        ms = jnp.mean(xf * xf, axis=-1, keepdims=True)
        h = (xf * lax.rsqrt(ms + NORM_EPS) * g_ref[...]).astype(BF16)
        h_ref[...] = h
        if has_small:
            small_ref[...] = jnp.dot(h, ws_ref[...], preferred_element_type=F32)

    acc = jnp.dot(h_ref[...], w_ref[...], preferred_element_type=F32)
    for (lo, hi), scale, o_ref in zip(bounds, scales, out_refs):
        @pl.when((j >= lo) & (j < hi))
        def _(scale=scale, o_ref=o_ref):
            val = acc if scale is None else acc * scale
            o_ref[...] = val.astype(o_ref.dtype)


def _norm_matmul(x, g, w, segs, *, tm, tn, w_small=None):
    m, k = x.shape
    assert m % tm == 0
    bounds = []
    lo = 0
    for ncols, _, _ in segs:
        assert ncols % tn == 0
        bounds.append((lo // tn, (lo + ncols) // tn))
        lo += ncols
    n = lo
    assert n <= w.shape[1]

    in_specs = [
        pl.BlockSpec((tm, k), lambda i, j: (i, 0)),
        pl.BlockSpec((1, k), lambda i, j: (0, 0)),
        pl.BlockSpec((k, tn), lambda i, j: (0, j)),
    ]
    args = [x, g.reshape(1, k), w]
    if w_small is not None:
        in_specs.append(pl.BlockSpec((k, w_small.shape[1]), lambda i, j: (0, 0)))
        args.append(w_small)

    out_specs = []
    out_shape = []
    for (blo, bhi), (ncols, dtype, _) in zip(bounds, segs):
        def idx(i, j, blo=blo, bhi=bhi):
            return (i, jnp.clip(j - blo, 0, bhi - blo - 1))
        out_specs.append(pl.BlockSpec((tm, tn), idx))
        out_shape.append(jax.ShapeDtypeStruct((m, ncols), dtype))
    if w_small is not None:
        ns = w_small.shape[1]
        out_specs.append(pl.BlockSpec((tm, ns), lambda i, j: (i, 0)))
        out_shape.append(jax.ShapeDtypeStruct((m, ns), F32))

    return pl.pallas_call(
        functools.partial(_norm_matmul_kernel, bounds=tuple(bounds),
                          scales=tuple(s for _, _, s in segs),
                          has_small=w_small is not None),
        grid=(m // tm, n // tn),
        in_specs=in_specs,
        out_specs=out_specs,
        out_shape=out_shape,
        scratch_shapes=[pltpu.VMEM((tm, k), BF16)],
        compiler_params=pltpu.CompilerParams(
            dimension_semantics=("arbitrary", "arbitrary"),
            vmem_limit_bytes=VMEM_LIMIT),
        name="norm_matmul",
    )(*args)


def _gmlp_kernel(u_ref, v_ref, z_ref, lng_ref, lnb_ref, ws_ref, sb_ref, o_ref,
                 vn_ref, *, rows):
    v = _gelu_tanh(v_ref[...])
    mu = jnp.mean(v, axis=-1, keepdims=True)
    vc = v - mu
    var = jnp.mean(vc * vc, axis=-1, keepdims=True)
    vn = vc * lax.rsqrt(var + NORM_EPS) * lng_ref[...] + lnb_ref[...]
    vn_ref[...] = vn.astype(BF16)

    row = lax.broadcasted_iota(jnp.int32, (CHUNK, CHUNK), 0)
    col = lax.broadcasted_iota(jnp.int32, (CHUNK, CHUNK), 1)
    causal = row >= col
    for g in range(GMLP_GROUPS):
        cs = slice(g * LANES, (g + 1) * LANES)
        w_s = jnp.where(causal, ws_ref[g], 0.0).astype(BF16)
        for c in range(rows // CHUNK):
            rs = slice(c * CHUNK, (c + 1) * CHUNK)
            v_mix = jnp.dot(w_s, vn_ref[rs, cs], preferred_element_type=F32)
            v_mix = v_mix + sb_ref[:, cs]
            u = _gelu_tanh(u_ref[rs, cs])
            o_ref[rs, cs] = (u * v_mix * _silu(z_ref[rs, cs])).astype(o_ref.dtype)


def _gmlp(uvz, ln_g, ln_b, spatial_w, spatial_b_exp, *, rows):
    m = uvz.shape[0]
    w = GMLP_WIDTH
    return pl.pallas_call(
        functools.partial(_gmlp_kernel, rows=rows),
        grid=(m // rows,),
        in_specs=[
            pl.BlockSpec((rows, w), lambda i: (i, 0)),
            pl.BlockSpec((rows, w), lambda i: (i, 1)),
            pl.BlockSpec((rows, w), lambda i: (i, 2)),
            pl.BlockSpec((1, w), lambda i: (0, 0)),
            pl.BlockSpec((1, w), lambda i: (0, 0)),
            pl.BlockSpec((GMLP_GROUPS, CHUNK, CHUNK), lambda i: (0, 0, 0)),
            pl.BlockSpec((CHUNK, w), lambda i: (0, 0)),
        ],
        out_specs=pl.BlockSpec((rows, w), lambda i: (i, 0)),
        out_shape=jax.ShapeDtypeStruct((m, w), BF16),
        scratch_shapes=[pltpu.VMEM((rows, w), BF16)],
        compiler_params=pltpu.CompilerParams(
            dimension_semantics=("arbitrary",), vmem_limit_bytes=VMEM_LIMIT),
        name="gmlp",
    )(uvz, uvz, uvz, ln_g.reshape(1, w), ln_b.reshape(1, w), spatial_w,
      spatial_b_exp)


def _ssd_kernel(zb_ref, xbc_ref, dt_ref, cw_ref, cb_ref, dtb_ref, alog_ref,
                dskip_ref, ng_ref, e_ref, o_ref, ext_ref, st_ref, y_ref):
    c = pl.program_id(1)

    @pl.when(c == 0)
    def _():
        ext_ref[0:HALO, :] = jnp.zeros((HALO, SSD_CONV_DIM), F32)
        st_ref[...] = jnp.zeros(st_ref.shape, F32)

    @pl.when(c != 0)
    def _():
        ext_ref[0:HALO, :] = ext_ref[CHUNK:CHUNK + HALO, :]

    ext_ref[HALO:HALO + CHUNK, :] = xbc_ref[...]

    ext = ext_ref[...]
    conv = cb_ref[...] + cw_ref[SSD_CONV - 1:SSD_CONV, :] * ext[HALO:]
    for k in range(SSD_CONV - 1):
        shift = SSD_CONV - 1 - k
        conv = conv + cw_ref[k:k + 1, :] * pltpu.roll(ext, shift, axis=0)[HALO:]
    xact = _silu(conv)
    xs = xact[:, :SSD_WIDTH]

    dt = _softplus(dt_ref[...] + dtb_ref[...])
    adt = dt * (-jnp.exp(alog_ref[...]))
    row = lax.broadcasted_iota(jnp.int32, (CHUNK, CHUNK), 0)
    col = lax.broadcasted_iota(jnp.int32, (CHUNK, CHUNK), 1)
    causal = row >= col
    tri = jnp.where(causal, 1.0, 0.0).astype(BF16)
    a_cum = jnp.dot(jnp.concatenate([tri] * SPLIT_TERMS, axis=1),
                    jnp.concatenate([t.astype(BF16) for t in _split_f32(adt)], axis=0),
                    preferred_element_type=F32)
    a_cum_t = a_cum.T

    stacked = jnp.concatenate([dt, a_cum], axis=0)
    terms = _split_f32(stacked)
    copy = lax.broadcasted_iota(jnp.int32, stacked.shape, 1) // SSD_HEADS
    packed = jnp.where(copy == 0, terms[0], jnp.where(copy == 1, terms[1], terms[2]))
    expanded = jnp.dot(packed.astype(BF16), e_ref[...],
                       preferred_element_type=F32)
    dt_exp = expanded[:CHUNK]
    acum_exp = expanded[CHUNK:]
    alast_exp = acum_exp[CHUNK - 1:CHUNK, :]
    exp_a = jnp.exp(acum_exp)
    decay_to_end = jnp.exp(alast_exp - acum_exp)
    chunk_decay = jnp.exp(alast_exp)

    xdt = xs * dt_exp
    lane = lax.broadcasted_iota(jnp.int32, (CHUNK, SSD_WIDTH), 1)
    low_half = (lane % LANES) < SSD_HEAD_DIM
    xdt_lo = jnp.where(low_half, xdt, 0.0).astype(BF16)
    xdt_hi = jnp.where(low_half, 0.0, xdt).astype(BF16)
    xdte = (xdt * decay_to_end).astype(BF16)

    heads_per_group = SSD_HEADS // SSD_GROUPS
    for g in range(SSD_GROUPS):
        gs = slice(g * SSD_GROUP_WIDTH, (g + 1) * SSD_GROUP_WIDTH)
        b_g = xact[:, SSD_WIDTH + g * SSD_STATE:SSD_WIDTH + (g + 1) * SSD_STATE]
        c_g = xact[:, SSD_WIDTH + SSD_BC_DIM + g * SSD_STATE:
                   SSD_WIDTH + SSD_BC_DIM + (g + 1) * SSD_STATE].astype(BF16)
        b_gt = b_g.T.astype(BF16)
        cb = jnp.dot(c_g, b_gt, preferred_element_type=F32)

        st_old = st_ref[g]
        y_off = jnp.dot(c_g, st_old.astype(BF16), preferred_element_type=F32)
        y_off = y_off * exp_a[:, gs]

        for jp in range(heads_per_group // 2):
            blk = g * (heads_per_group // 2) + jp
            bs = slice(blk * LANES, (blk + 1) * LANES)
            ms = []
            for hh in range(2):
                h = 2 * blk + hh
                seg = a_cum[:, h:h + 1] - a_cum_t[h:h + 1, :]
                decay = jnp.exp(jnp.where(causal, seg, -jnp.inf))
                ms.append((cb * decay).astype(BF16))
            lhs = jnp.concatenate(ms, axis=1)
            rhs = jnp.concatenate([xdt_lo[:, bs], xdt_hi[:, bs]], axis=0)
            y_diag = jnp.dot(lhs, rhs, preferred_element_type=F32)
            y_ref[:, bs] = (y_diag + y_off[:, jp * LANES:(jp + 1) * LANES]
                            + dskip_ref[:, bs] * xs[:, bs])

        st_ref[g] = (st_old * chunk_decay[:, gs]
                     + jnp.dot(b_gt, xdte[:, gs], preferred_element_type=F32))

    for g in range(SSD_GROUPS):
        gs = slice(g * SSD_GROUP_WIDTH, (g + 1) * SSD_GROUP_WIDTH)
        yz = y_ref[:, gs] * _silu(zb_ref[:, gs])
        ms = jnp.mean(yz * yz, axis=-1, keepdims=True)
        o_ref[:, gs] = (yz * lax.rsqrt(ms + NORM_EPS) * ng_ref[:, gs]).astype(o_ref.dtype)


def _ssd(uvz, xbc, dt_raw, conv_w, conv_b, dt_bias_p, a_log_p, d_skip_exp,
         norm_g, expand, *, batch):
    m = uvz.shape[0]
    nc = m // batch // CHUNK
    full = lambda shape: pl.BlockSpec(shape, lambda b, c: (0,) * len(shape))
    return pl.pallas_call(
        _ssd_kernel,
        grid=(batch, nc),
        in_specs=[
            pl.BlockSpec((CHUNK, SSD_WIDTH), lambda b, c: (b * nc + c, 3)),
            pl.BlockSpec((CHUNK, SSD_CONV_DIM), lambda b, c: (b * nc + c, 0)),
            pl.BlockSpec((CHUNK, LANES), lambda b, c: (b * nc + c, 0)),
            full((SSD_CONV, SSD_CONV_DIM)),
            full((1, SSD_CONV_DIM)),
            full((1, LANES)),
            full((1, LANES)),
            full((1, SSD_WIDTH)),
            full((1, SSD_WIDTH)),
            full((LANES, SSD_WIDTH)),
        ],
        out_specs=pl.BlockSpec((CHUNK, SSD_WIDTH), lambda b, c: (b * nc + c, 0)),
        out_shape=jax.ShapeDtypeStruct((m, SSD_WIDTH), BF16),
        scratch_shapes=[
            pltpu.VMEM((CHUNK + HALO, SSD_CONV_DIM), F32),
            pltpu.VMEM((SSD_GROUPS, SSD_STATE, SSD_GROUP_WIDTH), F32),
            pltpu.VMEM((CHUNK, SSD_WIDTH), F32),
        ],
        compiler_params=pltpu.CompilerParams(
            dimension_semantics=("arbitrary", "arbitrary"),
            vmem_limit_bytes=VMEM_LIMIT),
        name="ssd",
    )(uvz, xbc, dt_raw, conv_w, conv_b.reshape(1, -1), dt_bias_p, a_log_p,
      d_skip_exp, norm_g.reshape(1, -1), expand)


def _out_proj2_kernel(ya_ref, yb_ref, wa_ref, wb_ref, x_ref, o_ref):
    acc = jnp.dot(ya_ref[...], wa_ref[...], preferred_element_type=F32)
    acc = acc + jnp.dot(yb_ref[...], wb_ref[...], preferred_element_type=F32)
    o_ref[...] = x_ref[...] + acc


def _out_proj2(ya, yb, w, x, *, tm, tn):
    m, k = ya.shape
    n = w.shape[1]
    assert w.shape[0] == 2 * k
    return pl.pallas_call(
        _out_proj2_kernel,
        grid=(m // tm, n // tn),
        in_specs=[
            pl.BlockSpec((tm, k), lambda i, j: (i, 0)),
            pl.BlockSpec((tm, k), lambda i, j: (i, 0)),
            pl.BlockSpec((k, tn), lambda i, j: (0, j)),
            pl.BlockSpec((k, tn), lambda i, j: (1, j)),
            pl.BlockSpec((tm, tn), lambda i, j: (i, j)),
        ],
        out_specs=pl.BlockSpec((tm, tn), lambda i, j: (i, j)),
        out_shape=jax.ShapeDtypeStruct((m, n), F32),
        compiler_params=pltpu.CompilerParams(
            dimension_semantics=("arbitrary", "arbitrary"),
            vmem_limit_bytes=VMEM_LIMIT),
        name="out_proj_even",
    )(ya, yb, w, w, x)


def _out_proj_norm_kernel(y_ref, w_ref, x_ref, g_ref, o_ref):
    x2 = x_ref[...] + jnp.dot(y_ref[...], w_ref[...], preferred_element_type=F32)
    ms = jnp.mean(x2 * x2, axis=-1, keepdims=True)
    o_ref[...] = x2 * lax.rsqrt(ms + NORM_EPS) * g_ref[...]


def _out_proj_norm(y, w, x, g, *, tm):
    m, k = y.shape
    n = w.shape[1]
    return pl.pallas_call(
        _out_proj_norm_kernel,
        grid=(m // tm,),
        in_specs=[
            pl.BlockSpec((tm, k), lambda i: (i, 0)),
            pl.BlockSpec((k, n), lambda i: (0, 0)),
            pl.BlockSpec((tm, n), lambda i: (i, 0)),
            pl.BlockSpec((1, n), lambda i: (0, 0)),
        ],
        out_specs=pl.BlockSpec((tm, n), lambda i: (i, 0)),
        out_shape=jax.ShapeDtypeStruct((m, n), F32),
        compiler_params=pltpu.CompilerParams(
            dimension_semantics=("arbitrary",), vmem_limit_bytes=VMEM_LIMIT),
        name="out_proj_odd_norm",
    )(y, w, x, g.reshape(1, n))


ATTN_PREP_ROWS = 256


def _diff_attn_kernel(slopes_ref, q_ref, k_ref, v_ref, gate_ref, lq1_ref, lk1_ref,
                      lq2_ref, lk2_ref, sg_ref, o_ref, ka_ref, kb_ref, va_ref, *, tq):
    h = pl.program_id(1)

    def build_operands():
        slope2 = slopes_ref[h] * LOG2E
        shape = (ATTN_PREP_ROWS, LANES)
        lane = lax.broadcasted_iota(jnp.int32, shape, 1)
        first_map = lane < DIFF_HEAD_DIM
        sub = lane % DIFF_HEAD_DIM
        rows = lax.broadcasted_iota(jnp.int32, shape, 0)

        def prep(ci, _):
            r0 = pl.multiple_of(ci * ATTN_PREP_ROWS, ATTN_PREP_ROWS)
            t1, t2, t3 = _split_f32(slope2 * (rows + r0).astype(F32))
            tab = jnp.where(sub == 0, t1, jnp.where(sub == 1, t2,
                            jnp.where(sub == 2, t3, 0.0)))
            k = k_ref[pl.ds(r0, ATTN_PREP_ROWS), :].astype(F32)
            ka_ref[pl.ds(r0, ATTN_PREP_ROWS), :] = jnp.where(first_map, k, tab).astype(BF16)
            kb_ref[pl.ds(r0, ATTN_PREP_ROWS), :] = jnp.where(first_map, tab, k).astype(BF16)
            va_ref[pl.ds(r0, ATTN_PREP_ROWS), 0:DIFF_V_DIM] = v_ref[pl.ds(r0, ATTN_PREP_ROWS), :]
            va_ref[pl.ds(r0, ATTN_PREP_ROWS), DIFF_V_DIM:2 * DIFF_V_DIM] = (
                jnp.ones((ATTN_PREP_ROWS, DIFF_V_DIM), BF16))
            return 0

        lax.fori_loop(0, SEQ // ATTN_PREP_ROWS, prep, 0)

    build_operands()

    lane = lax.broadcasted_iota(jnp.int32, (tq, LANES), 1)
    first_map = lane < DIFF_HEAD_DIM
    ones_mask = jnp.where((lane % DIFF_HEAD_DIM) < SPLIT_TERMS, 1.0, 0.0)
    row = lax.broadcasted_iota(jnp.int32, (tq, tq), 0)
    col = lax.broadcasted_iota(jnp.int32, (tq, tq), 1)
    keep = row >= col
    nt_dims = (((1,), (1,)), ((), ()))
    lam = (jnp.exp(jnp.sum(lq1_ref[...] * lk1_ref[...], axis=-1, keepdims=True))
           - jnp.exp(jnp.sum(lq2_ref[...] * lk2_ref[...], axis=-1, keepdims=True))
           + LAMBDA_INIT)

    def one_map(qq, k_ref, nk):
        s = lax.dot_general(qq, k_ref[0:nk, :], nt_dims, preferred_element_type=F32)
        s_diag = jnp.where(keep, s[:, nk - tq:], -jnp.inf)
        m = jnp.max(s_diag, axis=-1, keepdims=True)
        if nk > tq:
            s_past = s[:, :nk - tq]
            m = jnp.maximum(m, jnp.max(s_past, axis=-1, keepdims=True))
            p = jnp.concatenate([jnp.exp2(s_past - m), jnp.exp2(s_diag - m)], axis=1)
        else:
            p = jnp.exp2(s_diag - m)
        acc = jnp.dot(p.astype(BF16), va_ref[0:nk, :], preferred_element_type=F32)
        return acc[:, :DIFF_V_DIM] / acc[:, DIFF_V_DIM:DIFF_V_DIM + 1]

    for qi in range(SEQ // tq):
        qs = slice(qi * tq, (qi + 1) * tq)
        nk = (qi + 1) * tq
        qf = q_ref[qs, :].astype(F32)
        qa = jnp.where(first_map, qf, ones_mask).astype(BF16)
        qb = jnp.where(first_map, ones_mask, qf).astype(BF16)
        o = one_map(qa, ka_ref, nk) - lam * one_map(qb, kb_ref, nk)
        ms = jnp.mean(o * o, axis=-1, keepdims=True)
        o = o * lax.rsqrt(ms + NORM_EPS) * sg_ref[...]
        o = o * (1.0 - LAMBDA_INIT)
        o_ref[qs, :] = (o * _silu(gate_ref[qs, :])).astype(o_ref.dtype)


def _diff_attn(q, kv, gate, slopes, lq1, lk1, lq2, lk2, subln_g, *, batch, tq):
    m = q.shape[0]
    vec = lambda n: pl.BlockSpec((1, n), lambda b, h, s: (0, 0))
    grid_spec = pltpu.PrefetchScalarGridSpec(
        num_scalar_prefetch=1,
        grid=(batch, DIFF_HEADS),
        in_specs=[
            pl.BlockSpec((SEQ, LANES), lambda b, h, s: (b, h)),
            pl.BlockSpec((SEQ, LANES), lambda b, h, s: (b, h)),
            pl.BlockSpec((SEQ, LANES), lambda b, h, s: (b, DIFF_HEADS + h)),
            pl.BlockSpec((SEQ, LANES), lambda b, h, s: (b, h)),
            vec(DIFF_HEAD_DIM), vec(DIFF_HEAD_DIM), vec(DIFF_HEAD_DIM),
            vec(DIFF_HEAD_DIM), vec(DIFF_V_DIM),
        ],
        out_specs=pl.BlockSpec((SEQ, LANES), lambda b, h, s: (b, h)),
        scratch_shapes=[
            pltpu.VMEM((SEQ, LANES), BF16),
            pltpu.VMEM((SEQ, LANES), BF16),
            pltpu.VMEM((SEQ, 2 * DIFF_V_DIM), BF16),
        ],
    )
    return pl.pallas_call(
        functools.partial(_diff_attn_kernel, tq=tq),
        grid_spec=grid_spec,
        out_shape=jax.ShapeDtypeStruct((m, DIFF_WIDTH), BF16),
        compiler_params=pltpu.CompilerParams(
            dimension_semantics=("arbitrary", "arbitrary"),
            vmem_limit_bytes=VMEM_LIMIT),
        name="diff_attn",
    )(slopes, q, kv, kv, gate, lq1.reshape(1, -1), lk1.reshape(1, -1),
      lq2.reshape(1, -1), lk2.reshape(1, -1), subln_g.reshape(1, -1))


def kernel(x, l0_norm_g, l0_w_in, l0_gmlp_ln_g, l0_gmlp_ln_b, l0_spatial_w,
           l0_spatial_b, l0_conv_w, l0_conv_b, l0_dt_bias, l0_a_log, l0_d_skip,
           l0_ssm_norm_g, l0_w_out, l1_norm_g, l1_w_in, l1_lambda_q1,
           l1_lambda_k1, l1_lambda_q2, l1_lambda_k2, l1_subln_g, l1_w_out,
           final_norm_g):
    batch, seq, d = x.shape
    m = batch * seq
    x2d = x.reshape(m, d)

    def head_lanes(v):
        v = jnp.concatenate([v] * SPLIT_TERMS, axis=-1)
        pad = [(0, 0)] * (v.ndim - 1) + [(0, LANES - SPLIT_TERMS * SSD_HEADS)]
        return jnp.pad(v, pad)

    w_in0 = l0_w_in.astype(BF16)
    w_dt = head_lanes(l0_w_in[:, EVEN_MAIN:]).astype(BF16)
    uvz, xbc, dt_raw = _norm_matmul(
        x2d, l0_norm_g, w_in0, [(UVZ_WIDTH, F32, None), (SSD_CONV_DIM, F32, None)],
        tm=1024, tn=1024, w_small=w_dt)

    spatial_b_exp = jnp.repeat(l0_spatial_b.T, LANES, axis=1)
    y_a = _gmlp(uvz, l0_gmlp_ln_g, l0_gmlp_ln_b, l0_spatial_w, spatial_b_exp,
                rows=256)

    dt_bias_p = head_lanes(l0_dt_bias).reshape(1, LANES)
    a_log_p = head_lanes(l0_a_log).reshape(1, LANES)
    d_skip_exp = jnp.repeat(l0_d_skip, SSD_HEAD_DIM).reshape(1, SSD_WIDTH)
    lane_id = jnp.arange(LANES)
    expand = ((lane_id[:, None] % SSD_HEADS
               == (jnp.arange(SSD_WIDTH) // SSD_HEAD_DIM)[None, :])
              & (lane_id[:, None] < SPLIT_TERMS * SSD_HEADS)).astype(BF16)
    y_b = _ssd(uvz, xbc, dt_raw, l0_conv_w, l0_conv_b, dt_bias_p, a_log_p,
               d_skip_exp, l0_ssm_norm_g, expand, batch=batch)

    x1 = _out_proj2(y_a, y_b, l0_w_out.astype(BF16), x2d, tm=1024, tn=512)

    q_scale = LOG2E * DIFF_HEAD_DIM ** -0.5
    q, kv, gate = _norm_matmul(
        x1, l1_norm_g, l1_w_in.astype(BF16),
        [(DIFF_WIDTH, BF16, q_scale), (2 * DIFF_WIDTH, BF16, None),
         (DIFF_WIDTH, F32, None)], tm=1024, tn=1024)
    slopes = 2.0 ** (-8.0 * (jnp.arange(DIFF_HEADS, dtype=F32) + 1.0) / DIFF_HEADS)
    o = _diff_attn(q, kv, gate, slopes, l1_lambda_q1, l1_lambda_k1, l1_lambda_q2,
                   l1_lambda_k2, l1_subln_g, batch=batch, tq=256)

    out = _out_proj_norm(o, l1_w_out.astype(BF16), x1, final_norm_g, tm=512)
    return out.reshape(batch, seq, d)
```

```python
import functools
import math

import jax
import jax.numpy as jnp
from jax import lax
from jax.experimental import pallas as pl
from jax.experimental.pallas import tpu as pltpu

F32 = jnp.float32
BF16 = jnp.bfloat16

D_MODEL = 2048
SEQ = 2048
CHUNK = 128
NORM_EPS = 1e-5

GMLP_WIDTH = 2048
GMLP_GROUPS = 16
SSD_WIDTH = 2048
SSD_HEADS = 32
SSD_HEAD_DIM = 64
SSD_GROUPS = 4
SSD_STATE = 128
SSD_CONV = 4
SSD_BC_DIM = SSD_GROUPS * SSD_STATE
SSD_CONV_DIM = SSD_WIDTH + 2 * SSD_BC_DIM
SSD_GROUP_WIDTH = SSD_WIDTH // SSD_GROUPS
UVZ_WIDTH = 3 * GMLP_WIDTH + SSD_WIDTH
EVEN_MAIN = UVZ_WIDTH + SSD_CONV_DIM

DIFF_HEADS = 16
DIFF_HEAD_DIM = 64
DIFF_V_DIM = 128
DIFF_WIDTH = DIFF_HEADS * DIFF_V_DIM
LAMBDA_INIT = 0.8 - 0.6 * math.exp(-0.3 * 1)
LOG2E = math.log2(math.e)

LANES = 128
HALO = 8
VMEM_LIMIT = 56 * 1024 * 1024


def _silu(x):
    half = 0.5 * x
    return half + half * jnp.tanh(half)


SPLIT_TERMS = 3


def _split_f32(x):
    t1 = x.astype(BF16).astype(F32)
    r1 = x - t1
    t2 = r1.astype(BF16).astype(F32)
    t3 = (r1 - t2).astype(BF16).astype(F32)
    return t1, t2, t3


def _gelu_tanh(x):
    c = math.sqrt(2.0 / math.pi)
    return 0.5 * x * (1.0 + jnp.tanh(c * (x + 0.044715 * (x * x * x))))


def _softplus(x):
    return jnp.maximum(x, 0.0) + jnp.log1p(jnp.exp(-jnp.abs(x)))


def _norm_matmul_kernel(*refs, bounds, acts, has_small, w_transposed):
    dims = (((1,), (1,)), ((), ())) if w_transposed else (((1,), (0,)), ((), ()))
    refs = list(refs)
    x_ref, g_ref, w_ref = refs[:3]
    del refs[:3]
    ws_ref = refs.pop(0) if has_small else None
    out_refs = [refs.pop(0) for _ in bounds]
    small_ref = refs.pop(0) if has_small else None
    h_ref = refs.pop(0)

    j = pl.program_id(1)

    @pl.when(j == 0)
    def _():
        xf = x_ref[...]
        ms = jnp.mean(xf * xf, axis=-1, keepdims=True)
        h = (xf * lax.rsqrt(ms + NORM_EPS) * g_ref[...]).astype(BF16)
        h_ref[...] = h
        if has_small:
            small_ref[...] = lax.dot_general(h, ws_ref[...], dims,
                                             preferred_element_type=F32)

    branches = []
    for (lo, _), seg_acts, o_ref in zip(bounds, acts, out_refs):
        for nblk, act in seg_acts:
            branches.append((lo, lo + nblk, act, o_ref))
            lo += nblk
    for lo, hi, act, o_ref in branches:
        @pl.when((j >= lo) & (j < hi))
        def _(act=act, o_ref=o_ref):
            acc = lax.dot_general(h_ref[...], w_ref[...], dims, preferred_element_type=F32)
            if act is None:
                val = acc
            elif act == "gelu":
                val = _gelu_tanh(acc)
            elif act == "silu":
                val = _silu(acc)
            else:
                kind, c = act
                assert kind == "scale"
                val = acc * c
            o_ref[...] = val.astype(o_ref.dtype)


def _norm_matmul(x, g, w, segs, *, tm, tn, w_small=None, w_transposed=False):
    m, k = x.shape
    assert m % tm == 0
    bounds = []
    acts = []
    lo = 0
    for _, ranges in segs:
        start = lo
        for ncols, act in ranges:
            assert ncols % tn == 0
            lo += ncols
        bounds.append((start // tn, lo // tn))
        acts.append(tuple((ncols // tn, act) for ncols, act in ranges))
    n = lo
    n_axis = 0 if w_transposed else 1
    assert n <= w.shape[n_axis] and w.shape[1 - n_axis] == k

    if w_transposed:
        w_spec = pl.BlockSpec((tn, k), lambda i, j: (j, 0))
    else:
        w_spec = pl.BlockSpec((k, tn), lambda i, j: (0, j))
    in_specs = [
        pl.BlockSpec((tm, k), lambda i, j: (i, 0)),
        pl.BlockSpec((1, k), lambda i, j: (0, 0)),
        w_spec,
    ]
    args = [x, g.reshape(1, k), w]
    if w_small is not None:
        in_specs.append(pl.BlockSpec(w_small.shape, lambda i, j: (0, 0)))
        args.append(w_small)

    out_specs = []
    out_shape = []
    for (blo, bhi), (dtype, _) in zip(bounds, segs):
        def idx(i, j, blo=blo, bhi=bhi):
            return (i, jnp.clip(j - blo, 0, bhi - blo - 1))
        out_specs.append(pl.BlockSpec((tm, tn), idx))
        out_shape.append(jax.ShapeDtypeStruct((m, (bhi - blo) * tn), dtype))
    if w_small is not None:
        ns = w_small.shape[n_axis]
        out_specs.append(pl.BlockSpec((tm, ns), lambda i, j: (i, 0)))
        out_shape.append(jax.ShapeDtypeStruct((m, ns), F32))

    return pl.pallas_call(
        functools.partial(_norm_matmul_kernel, bounds=tuple(bounds),
                          acts=tuple(acts),
                          has_small=w_small is not None, w_transposed=w_transposed),
        grid=(m // tm, n // tn),
        in_specs=in_specs,
        out_specs=out_specs,
        out_shape=out_shape,
        scratch_shapes=[pltpu.VMEM((tm, k), BF16)],
        compiler_params=pltpu.CompilerParams(
            dimension_semantics=("arbitrary", "arbitrary"),
            vmem_limit_bytes=VMEM_LIMIT),
        name="norm_matmul",
    )(*args)


def _gmlp_kernel(u_ref, v_ref, z_ref, lng_ref, lnb_ref, ws_ref, sb_ref, o_ref,
                 vn_ref, *, rows):
    v = v_ref[...]
    mu = jnp.mean(v, axis=-1, keepdims=True)
    vc = v - mu
    var = jnp.mean(vc * vc, axis=-1, keepdims=True)
    vn = vc * lax.rsqrt(var + NORM_EPS) * lng_ref[...] + lnb_ref[...]
    vn_ref[...] = vn.astype(BF16)

    row = lax.broadcasted_iota(jnp.int32, (CHUNK, CHUNK), 0)
    col = lax.broadcasted_iota(jnp.int32, (CHUNK, CHUNK), 1)
    causal = row >= col
    for g in range(GMLP_GROUPS):
        cs = slice(g * LANES, (g + 1) * LANES)
        w_s = jnp.where(causal, ws_ref[g], 0.0).astype(BF16)
        for c in range(rows // CHUNK):
            rs = slice(c * CHUNK, (c + 1) * CHUNK)
            v_mix = jnp.dot(w_s, vn_ref[rs, cs], preferred_element_type=F32)
            v_mix = v_mix + sb_ref[:, cs]
            o_ref[rs, cs] = (u_ref[rs, cs] * v_mix * z_ref[rs, cs]).astype(o_ref.dtype)


def _gmlp(uvz, ln_g, ln_b, spatial_w, spatial_b_exp, *, rows):
    m = uvz.shape[0]
    w = GMLP_WIDTH
    return pl.pallas_call(
        functools.partial(_gmlp_kernel, rows=rows),
        grid=(m // rows,),
        in_specs=[
            pl.BlockSpec((rows, w), lambda i: (i, 0)),
            pl.BlockSpec((rows, w), lambda i: (i, 1)),
            pl.BlockSpec((rows, w), lambda i: (i, 2)),
            pl.BlockSpec((1, w), lambda i: (0, 0)),
            pl.BlockSpec((1, w), lambda i: (0, 0)),
            pl.BlockSpec((GMLP_GROUPS, CHUNK, CHUNK), lambda i: (0, 0, 0)),
            pl.BlockSpec((CHUNK, w), lambda i: (0, 0)),
        ],
        out_specs=pl.BlockSpec((rows, w), lambda i: (i, 0)),
        out_shape=jax.ShapeDtypeStruct((m, w), BF16),
        scratch_shapes=[pltpu.VMEM((rows, w), BF16)],
        compiler_params=pltpu.CompilerParams(
            dimension_semantics=("arbitrary",), vmem_limit_bytes=VMEM_LIMIT),
        name="gmlp",
    )(uvz, uvz, uvz, ln_g.reshape(1, w), ln_b.reshape(1, w), spatial_w,
      spatial_b_exp)


def _ssd_kernel(zb_ref, xbc_ref, dt_ref, cw_ref, cb_ref, dtb_ref, alog_ref,
                dskip_ref, ng_ref, e_ref, o_ref, ext_ref, st_ref, y_ref):
    c = pl.program_id(1)

    @pl.when(c == 0)
    def _():
        ext_ref[0:HALO, :] = jnp.zeros((HALO, SSD_CONV_DIM), F32)
        st_ref[...] = jnp.zeros(st_ref.shape, F32)

    @pl.when(c != 0)
    def _():
        ext_ref[0:HALO, :] = ext_ref[CHUNK:CHUNK + HALO, :]

    ext_ref[HALO:HALO + CHUNK, :] = xbc_ref[...]

    ext = ext_ref[...]
    conv = cb_ref[...] + cw_ref[SSD_CONV - 1:SSD_CONV, :] * ext[HALO:]
    for k in range(SSD_CONV - 1):
        shift = SSD_CONV - 1 - k
        conv = conv + cw_ref[k:k + 1, :] * pltpu.roll(ext, shift, axis=0)[HALO:]
    xact = _silu(conv)
    xs = xact[:, :SSD_WIDTH]

    dt = _softplus(dt_ref[...] + dtb_ref[...])
    adt = dt * (-jnp.exp(alog_ref[...]))
    row = lax.broadcasted_iota(jnp.int32, (CHUNK, CHUNK), 0)
    col = lax.broadcasted_iota(jnp.int32, (CHUNK, CHUNK), 1)
    causal = row >= col
    tri = jnp.where(causal, 1.0, 0.0).astype(BF16)
    a_cum = jnp.dot(jnp.concatenate([tri] * SPLIT_TERMS, axis=1),
                    jnp.concatenate([t.astype(BF16) for t in _split_f32(adt)], axis=0),
                    preferred_element_type=F32)
    a_cum_t = a_cum.T

    stacked = jnp.concatenate([dt, a_cum], axis=0)
    terms = _split_f32(stacked)
    copy = lax.broadcasted_iota(jnp.int32, stacked.shape, 1) // SSD_HEADS
    packed = jnp.where(copy == 0, terms[0], jnp.where(copy == 1, terms[1], terms[2]))
    expanded = jnp.dot(packed.astype(BF16), e_ref[...],
                       preferred_element_type=F32)
    dt_exp = expanded[:CHUNK]
    acum_exp = expanded[CHUNK:]
    alast_exp = acum_exp[CHUNK - 1:CHUNK, :]
    exp_a = jnp.exp(acum_exp)
    decay_to_end = jnp.exp(alast_exp - acum_exp)
    chunk_decay = jnp.exp(alast_exp)

    xdt = xs * dt_exp
    lane = lax.broadcasted_iota(jnp.int32, (CHUNK, SSD_WIDTH), 1)
    low_half = (lane % LANES) < SSD_HEAD_DIM
    xdt_lo = jnp.where(low_half, xdt, 0.0).astype(BF16)
    xdt_hi = jnp.where(low_half, 0.0, xdt).astype(BF16)
    xdte = (xdt * decay_to_end).astype(BF16)

    heads_per_group = SSD_HEADS // SSD_GROUPS
    for g in range(SSD_GROUPS):
        gs = slice(g * SSD_GROUP_WIDTH, (g + 1) * SSD_GROUP_WIDTH)
        b_g = xact[:, SSD_WIDTH + g * SSD_STATE:SSD_WIDTH + (g + 1) * SSD_STATE]
        c_g = xact[:, SSD_WIDTH + SSD_BC_DIM + g * SSD_STATE:
                   SSD_WIDTH + SSD_BC_DIM + (g + 1) * SSD_STATE].astype(BF16)
        b_gt = b_g.T.astype(BF16)
        cb = jnp.dot(c_g, b_gt, preferred_element_type=F32)

        st_old = st_ref[g]
        y_off = jnp.dot(c_g, st_old.astype(BF16), preferred_element_type=F32)
        y_off = y_off * exp_a[:, gs]

        for jp in range(heads_per_group // 2):
            blk = g * (heads_per_group // 2) + jp
            bs = slice(blk * LANES, (blk + 1) * LANES)
            ms = []
            for hh in range(2):
                h = 2 * blk + hh
                seg = a_cum[:, h:h + 1] - a_cum_t[h:h + 1, :]
                decay = jnp.exp(jnp.where(causal, seg, -jnp.inf))
                ms.append((cb * decay).astype(BF16))
            lhs = jnp.concatenate(ms, axis=1)
            rhs = jnp.concatenate([xdt_lo[:, bs], xdt_hi[:, bs]], axis=0)
            y_diag = jnp.dot(lhs, rhs, preferred_element_type=F32)
            y_ref[:, bs] = (y_diag + y_off[:, jp * LANES:(jp + 1) * LANES]
                            + dskip_ref[:, bs] * xs[:, bs])

        st_ref[g] = (st_old * chunk_decay[:, gs]
                     + jnp.dot(b_gt, xdte[:, gs], preferred_element_type=F32))

    for g in range(SSD_GROUPS):
        gs = slice(g * SSD_GROUP_WIDTH, (g + 1) * SSD_GROUP_WIDTH)
        yz = y_ref[:, gs] * zb_ref[:, gs]
        ms = jnp.mean(yz * yz, axis=-1, keepdims=True)
        o_ref[:, gs] = (yz * lax.rsqrt(ms + NORM_EPS) * ng_ref[:, gs]).astype(o_ref.dtype)


def _ssd(uvz, xbc, dt_raw, conv_w, conv_b, dt_bias_p, a_log_p, d_skip_exp, norm_g,
         expand, *, batch):
    m = uvz.shape[0]
    nc = m // batch // CHUNK
    full = lambda shape: pl.BlockSpec(shape, lambda b, c: (0,) * len(shape))
    return pl.pallas_call(
        _ssd_kernel,
        grid=(batch, nc),
        in_specs=[
            pl.BlockSpec((CHUNK, SSD_WIDTH), lambda b, c: (b * nc + c, 3)),
            pl.BlockSpec((CHUNK, SSD_CONV_DIM), lambda b, c: (b * nc + c, 0)),
            pl.BlockSpec((CHUNK, LANES), lambda b, c: (b * nc + c, 0)),
            full((SSD_CONV, SSD_CONV_DIM)),
            full((1, SSD_CONV_DIM)),
            full((1, LANES)),
            full((1, LANES)),
            full((1, SSD_WIDTH)),
            full((1, SSD_WIDTH)),
            full((LANES, SSD_WIDTH)),
        ],
        out_specs=pl.BlockSpec((CHUNK, SSD_WIDTH), lambda b, c: (b * nc + c, 0)),
        out_shape=jax.ShapeDtypeStruct((m, SSD_WIDTH), BF16),
        scratch_shapes=[
            pltpu.VMEM((CHUNK + HALO, SSD_CONV_DIM), F32),
            pltpu.VMEM((SSD_GROUPS, SSD_STATE, SSD_GROUP_WIDTH), F32),
            pltpu.VMEM((CHUNK, SSD_WIDTH), F32),
        ],
        compiler_params=pltpu.CompilerParams(
            dimension_semantics=("arbitrary", "arbitrary"),
            vmem_limit_bytes=VMEM_LIMIT),
        name="ssd",
    )(uvz, xbc, dt_raw, conv_w, conv_b.reshape(1, -1), dt_bias_p, a_log_p, d_skip_exp,
      norm_g.reshape(1, -1), expand)


def _out_proj2_kernel(ya_ref, yb_ref, wa_ref, wb_ref, x_ref, o_ref):
    acc = jnp.dot(ya_ref[...], wa_ref[...], preferred_element_type=F32)
    acc = acc + jnp.dot(yb_ref[...], wb_ref[...], preferred_element_type=F32)
    o_ref[...] = x_ref[...] + acc


def _out_proj2(ya, yb, w, x, *, tm, tn):
    m, k = ya.shape
    n = w.shape[1]
    assert w.shape[0] == 2 * k
    return pl.pallas_call(
        _out_proj2_kernel,
        grid=(m // tm, n // tn),
        in_specs=[
            pl.BlockSpec((tm, k), lambda i, j: (i, 0)),
            pl.BlockSpec((tm, k), lambda i, j: (i, 0)),
            pl.BlockSpec((k, tn), lambda i, j: (0, j)),
            pl.BlockSpec((k, tn), lambda i, j: (1, j)),
            pl.BlockSpec((tm, tn), lambda i, j: (i, j)),
        ],
        out_specs=pl.BlockSpec((tm, tn), lambda i, j: (i, j)),
        out_shape=jax.ShapeDtypeStruct((m, n), F32),
        compiler_params=pltpu.CompilerParams(
            dimension_semantics=("arbitrary", "arbitrary"),
            vmem_limit_bytes=VMEM_LIMIT),
        name="out_proj_even",
    )(ya, yb, w, w, x)


def _out_proj_norm_kernel(y_ref, w_ref, x_ref, g_ref, o_ref):
    x2 = x_ref[...] + jnp.dot(y_ref[...], w_ref[...], preferred_element_type=F32)
    ms = jnp.mean(x2 * x2, axis=-1, keepdims=True)
    o_ref[...] = x2 * lax.rsqrt(ms + NORM_EPS) * g_ref[...]


def _out_proj_norm(y, w, x, g, *, tm):
    m, k = y.shape
    n = w.shape[1]
    return pl.pallas_call(
        _out_proj_norm_kernel,
        grid=(m // tm,),
        in_specs=[
            pl.BlockSpec((tm, k), lambda i: (i, 0)),
            pl.BlockSpec((k, n), lambda i: (0, 0)),
            pl.BlockSpec((tm, n), lambda i: (i, 0)),
            pl.BlockSpec((1, n), lambda i: (0, 0)),
        ],
        out_specs=pl.BlockSpec((tm, n), lambda i: (i, 0)),
        out_shape=jax.ShapeDtypeStruct((m, n), F32),
        compiler_params=pltpu.CompilerParams(
            dimension_semantics=("arbitrary",), vmem_limit_bytes=VMEM_LIMIT),
        name="out_proj_odd_norm",
    )(y, w, x, g.reshape(1, n))


def _diff_attn_kernel(slopes_ref, q_ref, k_ref, v_ref, gate_ref, lq1_ref, lk1_ref,
                      lq2_ref, lk2_ref, sg_ref, o_ref, ka_ref, kb_ref, va_ref, *, tq):
    h = pl.program_id(1)

    @pl.when((pl.program_id(0) == 0) & (h == 0))
    def _():
        va_ref[:, DIFF_V_DIM:2 * DIFF_V_DIM] = jnp.ones((SEQ, DIFF_V_DIM), BF16)

    slope2 = slopes_ref[h] * LOG2E
    lane = lax.broadcasted_iota(jnp.int32, (tq, LANES), 1)
    first_map = lane < DIFF_HEAD_DIM
    sub = lane % DIFF_HEAD_DIM
    rows = lax.broadcasted_iota(jnp.int32, (tq, LANES), 0)
    ones_mask = jnp.where(sub < SPLIT_TERMS, 1.0, 0.0)

    def build_key_operands(rs):
        t1, t2, t3 = _split_f32(slope2 * (rows + rs.start).astype(F32))
        tab = jnp.where(sub == 0, t1, jnp.where(sub == 1, t2, jnp.where(sub == 2, t3, 0.0)))
        k = k_ref[rs, :].astype(F32)
        ka_ref[rs, :] = jnp.where(first_map, k, tab).astype(BF16)
        kb_ref[rs, :] = jnp.where(first_map, tab, k).astype(BF16)
        va_ref[rs, 0:DIFF_V_DIM] = v_ref[rs, :]

    row = lax.broadcasted_iota(jnp.int32, (tq, tq), 0)
    col = lax.broadcasted_iota(jnp.int32, (tq, tq), 1)
    keep = row >= col
    nt_dims = (((1,), (1,)), ((), ()))
    lam = (jnp.exp(jnp.sum(lq1_ref[...] * lk1_ref[...], axis=-1, keepdims=True))
           - jnp.exp(jnp.sum(lq2_ref[...] * lk2_ref[...], axis=-1, keepdims=True))
           + LAMBDA_INIT)

    def probs(qq, k_ref, nk):
        s = lax.dot_general(qq, k_ref[0:nk, :], nt_dims, preferred_element_type=F32)
        s_diag = jnp.where(keep, s[:, nk - tq:], -jnp.inf)
        m = jnp.max(s_diag, axis=-1, keepdims=True)
        if nk == tq:
            return jnp.exp2(s_diag - m).astype(BF16)
        s_past = s[:, :nk - tq]
        m = jnp.maximum(m, jnp.max(s_past, axis=-1, keepdims=True))
        return jnp.concatenate([jnp.exp2(s_past - m), jnp.exp2(s_diag - m)],
                               axis=1).astype(BF16)

    for qi in range(SEQ // tq):
        qs = slice(qi * tq, (qi + 1) * tq)
        nk = (qi + 1) * tq
        build_key_operands(qs)
        qf = q_ref[qs, :].astype(F32)
        qa = jnp.where(first_map, qf, ones_mask).astype(BF16)
        qb = jnp.where(first_map, ones_mask, qf).astype(BF16)
        p = jnp.concatenate([probs(qa, ka_ref, nk), probs(qb, kb_ref, nk)], axis=0)
        acc = jnp.dot(p, va_ref[0:nk, :], preferred_element_type=F32)
        att = acc[:, :DIFF_V_DIM] / acc[:, DIFF_V_DIM:DIFF_V_DIM + 1]
        o = att[:tq] - lam * att[tq:]
        ms = jnp.mean(o * o, axis=-1, keepdims=True)
        o = o * lax.rsqrt(ms + NORM_EPS) * sg_ref[...]
        o = o * (1.0 - LAMBDA_INIT)
        o_ref[qs, :] = (o * _silu(gate_ref[qs, :])).astype(o_ref.dtype)


def _diff_attn(q, kv, gate, slopes, lq1, lk1, lq2, lk2, subln_g, *, batch, tq):
    m = q.shape[0]
    vec = lambda n: pl.BlockSpec((1, n), lambda b, h, s: (0, 0))
    grid_spec = pltpu.PrefetchScalarGridSpec(
        num_scalar_prefetch=1,
        grid=(batch, DIFF_HEADS),
        in_specs=[
            pl.BlockSpec((SEQ, LANES), lambda b, h, s: (b, h)),
            pl.BlockSpec((SEQ, LANES), lambda b, h, s: (b, h)),
            pl.BlockSpec((SEQ, LANES), lambda b, h, s: (b, DIFF_HEADS + h)),
            pl.BlockSpec((SEQ, LANES), lambda b, h, s: (b, h)),
            vec(DIFF_HEAD_DIM), vec(DIFF_HEAD_DIM), vec(DIFF_HEAD_DIM),
            vec(DIFF_HEAD_DIM), vec(DIFF_V_DIM),
        ],
        out_specs=pl.BlockSpec((SEQ, LANES), lambda b, h, s: (b, h)),
        scratch_shapes=[
            pltpu.VMEM((SEQ, LANES), BF16),
            pltpu.VMEM((SEQ, LANES), BF16),
            pltpu.VMEM((SEQ, 2 * DIFF_V_DIM), BF16),
        ],
    )
    return pl.pallas_call(
        functools.partial(_diff_attn_kernel, tq=tq),
        grid_spec=grid_spec,
        out_shape=jax.ShapeDtypeStruct((m, DIFF_WIDTH), BF16),
        compiler_params=pltpu.CompilerParams(
            dimension_semantics=("arbitrary", "arbitrary"),
            vmem_limit_bytes=VMEM_LIMIT),
        name="diff_attn",
    )(slopes, q, kv, kv, gate, lq1.reshape(1, -1), lk1.reshape(1, -1),
      lq2.reshape(1, -1), lk2.reshape(1, -1), subln_g.reshape(1, -1))


def kernel(x, l0_norm_g, l0_w_in, l0_gmlp_ln_g, l0_gmlp_ln_b, l0_spatial_w,
           l0_spatial_b, l0_conv_w, l0_conv_b, l0_dt_bias, l0_a_log, l0_d_skip,
           l0_ssm_norm_g, l0_w_out, l1_norm_g, l1_w_in, l1_lambda_q1,
           l1_lambda_k1, l1_lambda_q2, l1_lambda_k2, l1_subln_g, l1_w_out,
           final_norm_g):
    batch, seq, d = x.shape
    m = batch * seq
    x2d = x.reshape(m, d)

    def head_lanes(v):
        v = jnp.concatenate([v] * SPLIT_TERMS, axis=0)
        return jnp.pad(v, [(0, LANES - SPLIT_TERMS * SSD_HEADS)] + [(0, 0)] * (v.ndim - 1))

    w_in0_t = l0_w_in.T.astype(BF16)
    w_dt_t = head_lanes(w_in0_t[EVEN_MAIN:])
    uvz, xbc, dt_raw = _norm_matmul(
        x2d, l0_norm_g, w_in0_t,
        [(F32, [(2 * GMLP_WIDTH, "gelu"), (GMLP_WIDTH + SSD_WIDTH, "silu")]),
         (F32, [(SSD_CONV_DIM, None)])],
        tm=1024, tn=1024, w_small=w_dt_t, w_transposed=True)

    spatial_b_exp = jnp.repeat(l0_spatial_b.T, LANES, axis=1)
    y_a = _gmlp(uvz, l0_gmlp_ln_g, l0_gmlp_ln_b, l0_spatial_w, spatial_b_exp,
                rows=512)

    dt_bias_p = head_lanes(l0_dt_bias).reshape(1, LANES)
    a_log_p = head_lanes(l0_a_log).reshape(1, LANES)
    d_skip_exp = jnp.repeat(l0_d_skip, SSD_HEAD_DIM).reshape(1, SSD_WIDTH)
    lane_id = jnp.arange(LANES)
    expand = ((lane_id[:, None] % SSD_HEADS
               == (jnp.arange(SSD_WIDTH) // SSD_HEAD_DIM)[None, :])
              & (lane_id[:, None] < SPLIT_TERMS * SSD_HEADS)).astype(BF16)
    y_b = _ssd(uvz, xbc, dt_raw, l0_conv_w, l0_conv_b, dt_bias_p, a_log_p, d_skip_exp,
               l0_ssm_norm_g, expand, batch=batch)

    x1 = _out_proj2(y_a, y_b, l0_w_out.astype(BF16), x2d, tm=1024, tn=1024)

    q_scale = LOG2E * DIFF_HEAD_DIM ** -0.5
    q, kv, gate = _norm_matmul(
        x1, l1_norm_g, l1_w_in.astype(BF16),
        [(BF16, [(DIFF_WIDTH, ("scale", q_scale))]), (BF16, [(2 * DIFF_WIDTH, None)]),
         (F32, [(DIFF_WIDTH, None)])], tm=1024, tn=1024)
    slopes = 2.0 ** (-8.0 * (jnp.arange(DIFF_HEADS, dtype=F32) + 1.0) / DIFF_HEADS)
    o = _diff_attn(q, kv, gate, slopes, l1_lambda_q1, l1_lambda_k1, l1_lambda_q2,
                   l1_lambda_k2, l1_subln_g, batch=batch, tq=256)

    out = _out_proj_norm(o, l1_w_out.astype(BF16), x1, final_norm_g, tm=512)
    return out.reshape(batch, seq, d)
```

```python
import functools
import math

import jax
import jax.numpy as jnp
from jax import lax
from jax.experimental import pallas as pl
from jax.experimental.pallas import tpu as pltpu

F32 = jnp.float32
BF16 = jnp.bfloat16

D_MODEL = 2048
SEQ = 2048
CHUNK = 128
NORM_EPS = 1e-5

GMLP_WIDTH = 2048
GMLP_GROUPS = 16
SSD_WIDTH = 2048
SSD_HEADS = 32
SSD_HEAD_DIM = 64
SSD_GROUPS = 4
SSD_STATE = 128
SSD_CONV = 4
SSD_BC_DIM = SSD_GROUPS * SSD_STATE
SSD_CONV_DIM = SSD_WIDTH + 2 * SSD_BC_DIM
SSD_GROUP_WIDTH = SSD_WIDTH // SSD_GROUPS
UVZ_WIDTH = 3 * GMLP_WIDTH + SSD_WIDTH
EVEN_MAIN = UVZ_WIDTH + SSD_CONV_DIM

DIFF_HEADS = 16
DIFF_HEAD_DIM = 64
DIFF_V_DIM = 128
DIFF_WIDTH = DIFF_HEADS * DIFF_V_DIM
LAMBDA_INIT = 0.8 - 0.6 * math.exp(-0.3 * 1)
LOG2E = math.log2(math.e)

LANES = 128
HALO = 8
VMEM_LIMIT = 56 * 1024 * 1024


def _silu(x):
    half = 0.5 * x
    return half + half * jnp.tanh(half)


SPLIT_TERMS = 3


def _split_f32(x):
    t1 = x.astype(BF16).astype(F32)
    r1 = x - t1
    t2 = r1.astype(BF16).astype(F32)
    t3 = (r1 - t2).astype(BF16).astype(F32)
    return t1, t2, t3


def _gelu_tanh(x):
    c = math.sqrt(2.0 / math.pi)
    return 0.5 * x * (1.0 + jnp.tanh(c * (x + 0.044715 * (x * x * x))))


def _softplus(x):
    return jnp.maximum(x, 0.0) + jnp.log1p(jnp.exp(-jnp.abs(x)))


def _norm_matmul_kernel(*refs, bounds, acts, has_small, w_transposed):
    dims = (((1,), (1,)), ((), ())) if w_transposed else (((1,), (0,)), ((), ()))
    refs = list(refs)
    x_ref, g_ref, w_ref = refs[:3]
    del refs[:3]
    ws_ref = refs.pop(0) if has_small else None
    out_refs = [refs.pop(0) for _ in bounds]
    small_ref = refs.pop(0) if has_small else None
    h_ref = refs.pop(0)

    j = pl.program_id(1)

    @pl.when(j == 0)
    def _():
        xf = x_ref[...]
        ms = jnp.mean(xf * xf, axis=-1, keepdims=True)
        h = (xf * lax.rsqrt(ms + NORM_EPS) * g_ref[...]).astype(BF16)
        h_ref[...] = h
        if has_small:
            small_ref[...] = lax.dot_general(h, ws_ref[...], dims,
                                             preferred_element_type=F32)

    branches = []
    for (lo, _), seg_acts, o_ref in zip(bounds, acts, out_refs):
        for nblk, act in seg_acts:
            branches.append((lo, lo + nblk, act, o_ref))
            lo += nblk
    for lo, hi, act, o_ref in branches:
        @pl.when((j >= lo) & (j < hi))
        def _(act=act, o_ref=o_ref):
            acc = lax.dot_general(h_ref[...], w_ref[...], dims, preferred_element_type=F32)
            if act is None:
                val = acc
            elif act == "gelu":
                val = _gelu_tanh(acc)
            elif act == "silu":
                val = _silu(acc)
            else:
                kind, c = act
                assert kind == "scale"
                val = acc * c
            o_ref[...] = val.astype(o_ref.dtype)


def _norm_matmul(x, g, w, segs, *, tm, tn, w_small=None, w_transposed=False):
    m, k = x.shape
    assert m % tm == 0
    bounds = []
    acts = []
    lo = 0
    for _, ranges in segs:
        start = lo
        for ncols, act in ranges:
            assert ncols % tn == 0
            lo += ncols
        bounds.append((start // tn, lo // tn))
        acts.append(tuple((ncols // tn, act) for ncols, act in ranges))
    n = lo
    n_axis = 0 if w_transposed else 1
    assert n <= w.shape[n_axis] and w.shape[1 - n_axis] == k

    if w_transposed:
        w_spec = pl.BlockSpec((tn, k), lambda i, j: (j, 0))
    else:
        w_spec = pl.BlockSpec((k, tn), lambda i, j: (0, j))
    in_specs = [
        pl.BlockSpec((tm, k), lambda i, j: (i, 0)),
        pl.BlockSpec((1, k), lambda i, j: (0, 0)),
        w_spec,
    ]
    args = [x, g.reshape(1, k), w]
    if w_small is not None:
        in_specs.append(pl.BlockSpec(w_small.shape, lambda i, j: (0, 0)))
        args.append(w_small)

    out_specs = []
    out_shape = []
    for (blo, bhi), (dtype, _) in zip(bounds, segs):
        def idx(i, j, blo=blo, bhi=bhi):
            return (i, jnp.clip(j - blo, 0, bhi - blo - 1))
        out_specs.append(pl.BlockSpec((tm, tn), idx))
        out_shape.append(jax.ShapeDtypeStruct((m, (bhi - blo) * tn), dtype))
    if w_small is not None:
        ns = w_small.shape[n_axis]
        out_specs.append(pl.BlockSpec((tm, ns), lambda i, j: (i, 0)))
        out_shape.append(jax.ShapeDtypeStruct((m, ns), F32))

    return pl.pallas_call(
        functools.partial(_norm_matmul_kernel, bounds=tuple(bounds),
                          acts=tuple(acts),
                          has_small=w_small is not None, w_transposed=w_transposed),
        grid=(m // tm, n // tn),
        in_specs=in_specs,
        out_specs=out_specs,
        out_shape=out_shape,
        scratch_shapes=[pltpu.VMEM((tm, k), BF16)],
        compiler_params=pltpu.CompilerParams(
            dimension_semantics=("arbitrary", "arbitrary"),
            vmem_limit_bytes=VMEM_LIMIT),
        name="norm_matmul",
    )(*args)


def _gmlp_kernel(u_ref, v_ref, z_ref, lng_ref, lnb_ref, ws_ref, sb_ref, o_ref,
                 vn_ref, *, rows):
    v = v_ref[...].astype(F32)
    mu = jnp.mean(v, axis=-1, keepdims=True)
    vc = v - mu
    var = jnp.mean(vc * vc, axis=-1, keepdims=True)
    vn = vc * lax.rsqrt(var + NORM_EPS) * lng_ref[...] + lnb_ref[...]
    vn_ref[...] = vn.astype(BF16)

    row = lax.broadcasted_iota(jnp.int32, (CHUNK, CHUNK), 0)
    col = lax.broadcasted_iota(jnp.int32, (CHUNK, CHUNK), 1)
    causal = row >= col
    for g in range(GMLP_GROUPS):
        cs = slice(g * LANES, (g + 1) * LANES)
        w_s = jnp.where(causal, ws_ref[g], 0.0).astype(BF16)
        for c in range(rows // CHUNK):
            rs = slice(c * CHUNK, (c + 1) * CHUNK)
            v_mix = jnp.dot(w_s, vn_ref[rs, cs], preferred_element_type=F32)
            v_mix = v_mix + sb_ref[:, cs]
            o_ref[rs, cs] = (u_ref[rs, cs].astype(F32) * v_mix
                             * z_ref[rs, cs].astype(F32)).astype(o_ref.dtype)


def _gmlp(uvz, ln_g, ln_b, spatial_w, spatial_b_exp, *, rows):
    m = uvz.shape[0]
    w = GMLP_WIDTH
    return pl.pallas_call(
        functools.partial(_gmlp_kernel, rows=rows),
        grid=(m // rows,),
        in_specs=[
            pl.BlockSpec((rows, w), lambda i: (i, 0)),
            pl.BlockSpec((rows, w), lambda i: (i, 1)),
            pl.BlockSpec((rows, w), lambda i: (i, 2)),
            pl.BlockSpec((1, w), lambda i: (0, 0)),
            pl.BlockSpec((1, w), lambda i: (0, 0)),
            pl.BlockSpec((GMLP_GROUPS, CHUNK, CHUNK), lambda i: (0, 0, 0)),
            pl.BlockSpec((CHUNK, w), lambda i: (0, 0)),
        ],
        out_specs=pl.BlockSpec((rows, w), lambda i: (i, 0)),
        out_shape=jax.ShapeDtypeStruct((m, w), BF16),
        scratch_shapes=[pltpu.VMEM((rows, w), BF16)],
        compiler_params=pltpu.CompilerParams(
            dimension_semantics=("arbitrary",), vmem_limit_bytes=VMEM_LIMIT),
        name="gmlp",
    )(uvz, uvz, uvz, ln_g.reshape(1, w), ln_b.reshape(1, w), spatial_w,
      spatial_b_exp)


def _ssd_kernel(zb_ref, xbc_ref, dt_ref, cw_ref, cb_ref, dtb_ref, alog_ref,
                dskip_ref, ng_ref, e_ref, o_ref, ext_ref, st_ref, y_ref, *, rows):
    step = pl.program_id(1)

    @pl.when(step == 0)
    def _():
        ext_ref[0:HALO, :] = jnp.zeros((HALO, SSD_CONV_DIM), F32)
        st_ref[...] = jnp.zeros(st_ref.shape, F32)

    @pl.when(step != 0)
    def _():
        ext_ref[0:HALO, :] = ext_ref[rows:rows + HALO, :]

    ext_ref[HALO:HALO + rows, :] = xbc_ref[...]

    ext = ext_ref[...]
    conv = cb_ref[...] + cw_ref[SSD_CONV - 1:SSD_CONV, :] * ext[HALO:]
    for k in range(SSD_CONV - 1):
        shift = SSD_CONV - 1 - k
        conv = conv + cw_ref[k:k + 1, :] * pltpu.roll(ext, shift, axis=0)[HALO:]
    xact_all = _silu(conv)

    dt_all = _softplus(dt_ref[...] + dtb_ref[...])
    adt_all = dt_all * (-LOG2E * jnp.exp(alog_ref[...]))
    row = lax.broadcasted_iota(jnp.int32, (CHUNK, CHUNK), 0)
    col = lax.broadcasted_iota(jnp.int32, (CHUNK, CHUNK), 1)
    causal = row >= col
    tri = jnp.where(causal, 1.0, 0.0).astype(BF16)
    tri3 = jnp.concatenate([tri] * SPLIT_TERMS, axis=1)
    lane = lax.broadcasted_iota(jnp.int32, (CHUNK, SSD_WIDTH), 1)
    low_half = (lane % LANES) < SSD_HEAD_DIM
    copy = lax.broadcasted_iota(jnp.int32, (2 * CHUNK, LANES), 1) // SSD_HEADS
    heads_per_group = SSD_HEADS // SSD_GROUPS

    for ci in range(rows // CHUNK):
        rs = slice(ci * CHUNK, (ci + 1) * CHUNK)
        xact = xact_all[rs]
        xs = xact[:, :SSD_WIDTH]
        dt = dt_all[rs]
        a_cum = jnp.dot(tri3, jnp.concatenate(
            [t.astype(BF16) for t in _split_f32(adt_all[rs])], axis=0),
            preferred_element_type=F32)
        a_cum_t = a_cum.T

        terms = _split_f32(jnp.concatenate([dt, a_cum], axis=0))
        packed = jnp.where(copy == 0, terms[0], jnp.where(copy == 1, terms[1], terms[2]))
        expanded = jnp.dot(packed.astype(BF16), e_ref[...],
                           preferred_element_type=F32)
        dt_exp = expanded[:CHUNK]
        acum_exp = expanded[CHUNK:]
        alast_exp = acum_exp[CHUNK - 1:CHUNK, :]
        exp_a = jnp.exp2(acum_exp)
        decay_to_end = jnp.exp2(alast_exp - acum_exp)
        chunk_decay = jnp.exp2(alast_exp)

        xdt = xs * dt_exp
        xdt_lo = jnp.where(low_half, xdt, 0.0).astype(BF16)
        xdt_hi = jnp.where(low_half, 0.0, xdt).astype(BF16)
        xdte = (xdt * decay_to_end).astype(BF16)

        for g in range(SSD_GROUPS):
            gs = slice(g * SSD_GROUP_WIDTH, (g + 1) * SSD_GROUP_WIDTH)
            b_g = xact[:, SSD_WIDTH + g * SSD_STATE:SSD_WIDTH + (g + 1) * SSD_STATE]
            c_g = xact[:, SSD_WIDTH + SSD_BC_DIM + g * SSD_STATE:
                       SSD_WIDTH + SSD_BC_DIM + (g + 1) * SSD_STATE].astype(BF16)
            b_gt = b_g.T.astype(BF16)
            cb = jnp.dot(c_g, b_gt, preferred_element_type=F32)

            st_old = st_ref[g]
            y_off = jnp.dot(c_g, st_old.astype(BF16), preferred_element_type=F32)
            y_off = y_off * exp_a[:, gs]

            for jp in range(heads_per_group // 2):
                blk = g * (heads_per_group // 2) + jp
                bs = slice(blk * LANES, (blk + 1) * LANES)
                ms = []
                for hh in range(2):
                    h = 2 * blk + hh
                    seg = a_cum[:, h:h + 1] - a_cum_t[h:h + 1, :]
                    decay = jnp.exp2(jnp.where(causal, seg, -jnp.inf))
                    ms.append((cb * decay).astype(BF16))
                lhs = jnp.concatenate(ms, axis=1)
                rhs = jnp.concatenate([xdt_lo[:, bs], xdt_hi[:, bs]], axis=0)
                y_diag = jnp.dot(lhs, rhs, preferred_element_type=F32)
                y_ref[rs, bs] = (y_diag + y_off[:, jp * LANES:(jp + 1) * LANES]
                                 + dskip_ref[:, bs] * xs[:, bs])

            st_ref[g] = (st_old * chunk_decay[:, gs]
                         + jnp.dot(b_gt, xdte[:, gs], preferred_element_type=F32))

    for g in range(SSD_GROUPS):
        gs = slice(g * SSD_GROUP_WIDTH, (g + 1) * SSD_GROUP_WIDTH)
        yz = y_ref[:, gs] * zb_ref[:, gs].astype(F32)
        ms = jnp.mean(yz * yz, axis=-1, keepdims=True)
        o_ref[:, gs] = (yz * lax.rsqrt(ms + NORM_EPS) * ng_ref[:, gs]).astype(o_ref.dtype)


def _ssd(uvz, xbc, dt_raw, conv_w, conv_b, dt_bias_p, a_log_p, d_skip_exp, norm_g,
         expand, *, batch, rows):
    m = uvz.shape[0]
    ns = m // batch // rows
    full = lambda shape: pl.BlockSpec(shape, lambda b, c: (0,) * len(shape))
    return pl.pallas_call(
        functools.partial(_ssd_kernel, rows=rows),
        grid=(batch, ns),
        in_specs=[
            pl.BlockSpec((rows, SSD_WIDTH), lambda b, c: (b * ns + c, 3)),
            pl.BlockSpec((rows, SSD_CONV_DIM), lambda b, c: (b * ns + c, 0)),
            pl.BlockSpec((rows, LANES), lambda b, c: (b * ns + c, 0)),
            full((SSD_CONV, SSD_CONV_DIM)),
            full((1, SSD_CONV_DIM)),
            full((1, LANES)),
            full((1, LANES)),
            full((1, SSD_WIDTH)),
            full((1, SSD_WIDTH)),
            full((LANES, SSD_WIDTH)),
        ],
        out_specs=pl.BlockSpec((rows, SSD_WIDTH), lambda b, c: (b * ns + c, 0)),
        out_shape=jax.ShapeDtypeStruct((m, SSD_WIDTH), BF16),
        scratch_shapes=[
            pltpu.VMEM((rows + HALO, SSD_CONV_DIM), F32),
            pltpu.VMEM((SSD_GROUPS, SSD_STATE, SSD_GROUP_WIDTH), F32),
            pltpu.VMEM((rows, SSD_WIDTH), F32),
        ],
        compiler_params=pltpu.CompilerParams(
            dimension_semantics=("arbitrary", "arbitrary"),
            vmem_limit_bytes=VMEM_LIMIT),
        name="ssd",
    )(uvz, xbc, dt_raw, conv_w, conv_b.reshape(1, -1), dt_bias_p, a_log_p, d_skip_exp,
      norm_g.reshape(1, -1), expand)


def _out_proj2_kernel(ya_ref, yb_ref, wa_ref, wb_ref, x_ref, o_ref):
    acc = jnp.dot(ya_ref[...], wa_ref[...], preferred_element_type=F32)
    acc = acc + jnp.dot(yb_ref[...], wb_ref[...], preferred_element_type=F32)
    o_ref[...] = x_ref[...] + acc


def _out_proj2(ya, yb, w, x, *, tm, tn):
    m, k = ya.shape
    n = w.shape[1]
    assert w.shape[0] == 2 * k
    return pl.pallas_call(
        _out_proj2_kernel,
        grid=(m // tm, n // tn),
        in_specs=[
            pl.BlockSpec((tm, k), lambda i, j: (i, 0)),
            pl.BlockSpec((tm, k), lambda i, j: (i, 0)),
            pl.BlockSpec((k, tn), lambda i, j: (0, j)),
            pl.BlockSpec((k, tn), lambda i, j: (1, j)),
            pl.BlockSpec((tm, tn), lambda i, j: (i, j)),
        ],
        out_specs=pl.BlockSpec((tm, tn), lambda i, j: (i, j)),
        out_shape=jax.ShapeDtypeStruct((m, n), F32),
        compiler_params=pltpu.CompilerParams(
            dimension_semantics=("arbitrary", "arbitrary"),
            vmem_limit_bytes=VMEM_LIMIT),
        name="out_proj_even",
    )(ya, yb, w, w, x)


def _out_proj_norm_kernel(y_ref, w_ref, x_ref, g_ref, o_ref):
    x2 = x_ref[...] + jnp.dot(y_ref[...], w_ref[...], preferred_element_type=F32)
    ms = jnp.mean(x2 * x2, axis=-1, keepdims=True)
    o_ref[...] = x2 * lax.rsqrt(ms + NORM_EPS) * g_ref[...]


def _out_proj_norm(y, w, x, g, *, tm):
    m, k = y.shape
    n = w.shape[1]
    return pl.pallas_call(
        _out_proj_norm_kernel,
        grid=(m // tm,),
        in_specs=[
            pl.BlockSpec((tm, k), lambda i: (i, 0)),
            pl.BlockSpec((k, n), lambda i: (0, 0)),
            pl.BlockSpec((tm, n), lambda i: (i, 0)),
            pl.BlockSpec((1, n), lambda i: (0, 0)),
        ],
        out_specs=pl.BlockSpec((tm, n), lambda i: (i, 0)),
        out_shape=jax.ShapeDtypeStruct((m, n), F32),
        compiler_params=pltpu.CompilerParams(
            dimension_semantics=("arbitrary",), vmem_limit_bytes=VMEM_LIMIT),
        name="out_proj_odd_norm",
    )(y, w, x, g.reshape(1, n))


def _diff_attn_kernel(slopes_ref, q_ref, k_ref, v_ref, gate_ref, lq1_ref, lk1_ref,
                      lq2_ref, lk2_ref, sg_ref, o_ref, ka_ref, kb_ref, va_ref, *, tq, hps):
    hp = pl.program_id(1)

    @pl.when((pl.program_id(0) == 0) & (hp == 0))
    def _():
        for hh in range(hps):
            va_ref[hh, :, DIFF_V_DIM:2 * DIFF_V_DIM] = jnp.ones((SEQ, DIFF_V_DIM), BF16)

    lane = lax.broadcasted_iota(jnp.int32, (tq, LANES), 1)
    first_map = lane < DIFF_HEAD_DIM
    sub = lane % DIFF_HEAD_DIM
    rows = lax.broadcasted_iota(jnp.int32, (tq, LANES), 0)
    ones_mask = jnp.where(sub < SPLIT_TERMS, 1.0, 0.0)
    row = lax.broadcasted_iota(jnp.int32, (tq, tq), 0)
    col = lax.broadcasted_iota(jnp.int32, (tq, tq), 1)
    keep = row >= col
    nt_dims = (((1,), (1,)), ((), ()))
    lam = (jnp.exp(jnp.sum(lq1_ref[...] * lk1_ref[...], axis=-1, keepdims=True))
           - jnp.exp(jnp.sum(lq2_ref[...] * lk2_ref[...], axis=-1, keepdims=True))
           + LAMBDA_INIT)

    def probs(qq, kk, nk):
        s = lax.dot_general(qq, kk, nt_dims, preferred_element_type=F32)
        s_diag = jnp.where(keep, s[:, nk - tq:], -jnp.inf)
        m = jnp.max(s_diag, axis=-1, keepdims=True)
        if nk == tq:
            return jnp.exp2(s_diag - m).astype(BF16)
        s_past = s[:, :nk - tq]
        m = jnp.maximum(m, jnp.max(s_past, axis=-1, keepdims=True))
        return jnp.concatenate([jnp.exp2(s_past - m), jnp.exp2(s_diag - m)],
                               axis=1).astype(BF16)

    for qi in range(SEQ // tq):
        for hh in range(hps):
            hl = slice(hh * LANES, (hh + 1) * LANES)
            slope2 = slopes_ref[hp * hps + hh] * LOG2E
            qs = slice(qi * tq, (qi + 1) * tq)
            nk = (qi + 1) * tq
            t1, t2, t3 = _split_f32(slope2 * (rows + qs.start).astype(F32))
            tab = jnp.where(sub == 0, t1, jnp.where(sub == 1, t2, jnp.where(sub == 2, t3, 0.0)))
            k = k_ref[qs, hl].astype(F32)
            ka_ref[hh, qs, :] = jnp.where(first_map, k, tab).astype(BF16)
            kb_ref[hh, qs, :] = jnp.where(first_map, tab, k).astype(BF16)
            va_ref[hh, qs, 0:DIFF_V_DIM] = v_ref[qs, hl]

            qf = q_ref[qs, hl].astype(F32)
            qa = jnp.where(first_map, qf, ones_mask).astype(BF16)
            qb = jnp.where(first_map, ones_mask, qf).astype(BF16)
            p = jnp.concatenate([probs(qa, ka_ref[hh, 0:nk, :], nk),
                                 probs(qb, kb_ref[hh, 0:nk, :], nk)], axis=0)
            acc = jnp.dot(p, va_ref[hh, 0:nk, :], preferred_element_type=F32)
            att = acc[:, :DIFF_V_DIM] / acc[:, DIFF_V_DIM:DIFF_V_DIM + 1]
            o = att[:tq] - lam * att[tq:]
            ms = jnp.mean(o * o, axis=-1, keepdims=True)
            o = o * lax.rsqrt(ms + NORM_EPS) * sg_ref[...]
            o = o * (1.0 - LAMBDA_INIT)
            o_ref[qs, hl] = (o * gate_ref[qs, hl].astype(F32)).astype(o_ref.dtype)


def _diff_attn(q, kv, gate, slopes, lq1, lk1, lq2, lk2, subln_g, *, batch, tq, hps):
    m = q.shape[0]
    w = hps * LANES
    nb = DIFF_HEADS // hps
    vec = lambda n: pl.BlockSpec((1, n), lambda b, h, s: (0, 0))
    grid_spec = pltpu.PrefetchScalarGridSpec(
        num_scalar_prefetch=1,
        grid=(batch, nb),
        in_specs=[
            pl.BlockSpec((SEQ, w), lambda b, h, s: (b, h)),
            pl.BlockSpec((SEQ, w), lambda b, h, s: (b, h)),
            pl.BlockSpec((SEQ, w), lambda b, h, s: (b, nb + h)),
            pl.BlockSpec((SEQ, w), lambda b, h, s: (b, h)),
            vec(DIFF_HEAD_DIM), vec(DIFF_HEAD_DIM), vec(DIFF_HEAD_DIM),
            vec(DIFF_HEAD_DIM), vec(DIFF_V_DIM),
        ],
        out_specs=pl.BlockSpec((SEQ, w), lambda b, h, s: (b, h)),
        scratch_shapes=[
            pltpu.VMEM((hps, SEQ, LANES), BF16),
            pltpu.VMEM((hps, SEQ, LANES), BF16),
            pltpu.VMEM((hps, SEQ, 2 * DIFF_V_DIM), BF16),
        ],
    )
    return pl.pallas_call(
        functools.partial(_diff_attn_kernel, tq=tq, hps=hps),
        grid_spec=grid_spec,
        out_shape=jax.ShapeDtypeStruct((m, DIFF_WIDTH), BF16),
        compiler_params=pltpu.CompilerParams(
            dimension_semantics=("arbitrary", "arbitrary"),
            vmem_limit_bytes=VMEM_LIMIT),
        name="diff_attn",
    )(slopes, q, kv, kv, gate, lq1.reshape(1, -1), lk1.reshape(1, -1),
      lq2.reshape(1, -1), lk2.reshape(1, -1), subln_g.reshape(1, -1))


def kernel(x, l0_norm_g, l0_w_in, l0_gmlp_ln_g, l0_gmlp_ln_b, l0_spatial_w,
           l0_spatial_b, l0_conv_w, l0_conv_b, l0_dt_bias, l0_a_log, l0_d_skip,
           l0_ssm_norm_g, l0_w_out, l1_norm_g, l1_w_in, l1_lambda_q1,
           l1_lambda_k1, l1_lambda_q2, l1_lambda_k2, l1_subln_g, l1_w_out,
           final_norm_g):
    batch, seq, d = x.shape
    m = batch * seq
    x2d = x.reshape(m, d)

    def head_lanes(v):
        v = jnp.concatenate([v] * SPLIT_TERMS, axis=0)
        return jnp.pad(v, [(0, LANES - SPLIT_TERMS * SSD_HEADS)] + [(0, 0)] * (v.ndim - 1))

    w_in0_t = l0_w_in.T.astype(BF16)
    w_dt_t = head_lanes(w_in0_t[EVEN_MAIN:])
    uvz, xbc, dt_raw = _norm_matmul(
        x2d, l0_norm_g, w_in0_t,
        [(BF16, [(2 * GMLP_WIDTH, "gelu"), (GMLP_WIDTH + SSD_WIDTH, "silu")]),
         (F32, [(SSD_CONV_DIM, None)])],
        tm=1024, tn=1024, w_small=w_dt_t, w_transposed=True)

    spatial_b_exp = jnp.repeat(l0_spatial_b.T, LANES, axis=1)
    y_a = _gmlp(uvz, l0_gmlp_ln_g, l0_gmlp_ln_b, l0_spatial_w, spatial_b_exp,
                rows=512)

    dt_bias_p = head_lanes(l0_dt_bias).reshape(1, LANES)
    a_log_p = head_lanes(l0_a_log).reshape(1, LANES)
    d_skip_exp = jnp.repeat(l0_d_skip, SSD_HEAD_DIM).reshape(1, SSD_WIDTH)
    lane_id = jnp.arange(LANES)
    expand = ((lane_id[:, None] % SSD_HEADS
               == (jnp.arange(SSD_WIDTH) // SSD_HEAD_DIM)[None, :])
              & (lane_id[:, None] < SPLIT_TERMS * SSD_HEADS)).astype(BF16)
    y_b = _ssd(uvz, xbc, dt_raw, l0_conv_w, l0_conv_b, dt_bias_p, a_log_p, d_skip_exp,
               l0_ssm_norm_g, expand, batch=batch, rows=256)

    x1 = _out_proj2(y_a, y_b, l0_w_out.astype(BF16), x2d, tm=1024, tn=1024)

    q_scale = LOG2E * DIFF_HEAD_DIM ** -0.5
    q, kv, gate = _norm_matmul(
        x1, l1_norm_g, l1_w_in.astype(BF16),
        [(BF16, [(DIFF_WIDTH, ("scale", q_scale))]), (BF16, [(2 * DIFF_WIDTH, None)]),
         (BF16, [(DIFF_WIDTH, "silu")])], tm=1024, tn=1024)
    slopes = 2.0 ** (-8.0 * (jnp.arange(DIFF_HEADS, dtype=F32) + 1.0) / DIFF_HEADS)
    o = _diff_attn(q, kv, gate, slopes, l1_lambda_q1, l1_lambda_k1, l1_lambda_q2,
                   l1_lambda_k2, l1_subln_g, batch=batch, tq=256, hps=2)

    out = _out_proj_norm(o, l1_w_out.astype(BF16), x1, final_norm_g, tm=512)
    return out.reshape(batch, seq, d)
```

```python
import functools
import math

import jax
import jax.numpy as jnp
from jax import lax
from jax.experimental import pallas as pl
from jax.experimental.pallas import tpu as pltpu

F32 = jnp.float32
BF16 = jnp.bfloat16

D_MODEL = 2048
SEQ = 2048
CHUNK = 128
NORM_EPS = 1e-5

GMLP_WIDTH = 2048
GMLP_GROUPS = 16
SSD_WIDTH = 2048
SSD_HEADS = 32
SSD_HEAD_DIM = 64
SSD_GROUPS = 4
SSD_STATE = 128
SSD_CONV = 4
SSD_BC_DIM = SSD_GROUPS * SSD_STATE
SSD_CONV_DIM = SSD_WIDTH + 2 * SSD_BC_DIM
SSD_GROUP_WIDTH = SSD_WIDTH // SSD_GROUPS
UVZ_WIDTH = 3 * GMLP_WIDTH + SSD_WIDTH
EVEN_MAIN = UVZ_WIDTH + SSD_CONV_DIM

DIFF_HEADS = 16
DIFF_HEAD_DIM = 64
DIFF_V_DIM = 128
DIFF_WIDTH = DIFF_HEADS * DIFF_V_DIM
LAMBDA_INIT = 0.8 - 0.6 * math.exp(-0.3 * 1)
LOG2E = math.log2(math.e)

LANES = 128
HALO = 8
VMEM_LIMIT = 56 * 1024 * 1024


def _silu(x):
    half = 0.5 * x
    return half + half * jnp.tanh(half)


SPLIT_TERMS = 3


def _split_f32(x):
    t1 = x.astype(BF16).astype(F32)
    r1 = x - t1
    t2 = r1.astype(BF16).astype(F32)
    t3 = (r1 - t2).astype(BF16).astype(F32)
    return t1, t2, t3


def _gelu_tanh(x):
    c = math.sqrt(2.0 / math.pi)
    return 0.5 * x * (1.0 + jnp.tanh(c * (x + 0.044715 * (x * x * x))))


def _softplus(x):
    return jnp.maximum(x, 0.0) + jnp.log1p(jnp.exp(-jnp.abs(x)))


def _norm_matmul_kernel(*refs, bounds, acts, has_small, w_transposed):
    dims = (((1,), (1,)), ((), ())) if w_transposed else (((1,), (0,)), ((), ()))
    refs = list(refs)
    x_ref, g_ref, w_ref = refs[:3]
    del refs[:3]
    ws_ref = refs.pop(0) if has_small else None
    out_refs = [refs.pop(0) for _ in bounds]
    small_ref = refs.pop(0) if has_small else None
    h_ref = refs.pop(0)

    j = pl.program_id(1)

    @pl.when(j == 0)
    def _():
        xf = x_ref[...]
        ms = jnp.mean(xf * xf, axis=-1, keepdims=True)
        h = (xf * lax.rsqrt(ms + NORM_EPS) * g_ref[...]).astype(BF16)
        h_ref[...] = h
        if has_small:
            small_ref[...] = lax.dot_general(h, ws_ref[...], dims,
                                             preferred_element_type=F32)

    branches = []
    for (lo, _), seg_acts, o_ref in zip(bounds, acts, out_refs):
        for nblk, act in seg_acts:
            branches.append((lo, lo + nblk, act, o_ref))
            lo += nblk
    for lo, hi, act, o_ref in branches:
        @pl.when((j >= lo) & (j < hi))
        def _(act=act, o_ref=o_ref):
            acc = lax.dot_general(h_ref[...], w_ref[...], dims, preferred_element_type=F32)
            if act is None:
                val = acc
            elif act == "gelu":
                val = _gelu_tanh(acc)
            elif act == "silu":
                val = _silu(acc)
            else:
                kind, c = act
                assert kind == "scale"
                val = acc * c
            o_ref[...] = val.astype(o_ref.dtype)


def _norm_matmul(x, g, w, segs, *, tm, tn, w_small=None, w_transposed=False):
    m, k = x.shape
    assert m % tm == 0
    bounds = []
    acts = []
    lo = 0
    for _, ranges in segs:
        start = lo
        for ncols, act in ranges:
            assert ncols % tn == 0
            lo += ncols
        bounds.append((start // tn, lo // tn))
        acts.append(tuple((ncols // tn, act) for ncols, act in ranges))
    n = lo
    n_axis = 0 if w_transposed else 1
    assert n <= w.shape[n_axis] and w.shape[1 - n_axis] == k

    if w_transposed:
        w_spec = pl.BlockSpec((tn, k), lambda i, j: (j, 0))
    else:
        w_spec = pl.BlockSpec((k, tn), lambda i, j: (0, j))
    in_specs = [
        pl.BlockSpec((tm, k), lambda i, j: (i, 0)),
        pl.BlockSpec((1, k), lambda i, j: (0, 0)),
        w_spec,
    ]
    args = [x, g.reshape(1, k), w]
    if w_small is not None:
        in_specs.append(pl.BlockSpec(w_small.shape, lambda i, j: (0, 0)))
        args.append(w_small)

    out_specs = []
    out_shape = []
    for (blo, bhi), (dtype, _) in zip(bounds, segs):
        def idx(i, j, blo=blo, bhi=bhi):
            return (i, jnp.clip(j - blo, 0, bhi - blo - 1))
        out_specs.append(pl.BlockSpec((tm, tn), idx))
        out_shape.append(jax.ShapeDtypeStruct((m, (bhi - blo) * tn), dtype))
    if w_small is not None:
        ns = w_small.shape[n_axis]
        out_specs.append(pl.BlockSpec((tm, ns), lambda i, j: (i, 0)))
        out_shape.append(jax.ShapeDtypeStruct((m, ns), F32))

    return pl.pallas_call(
        functools.partial(_norm_matmul_kernel, bounds=tuple(bounds),
                          acts=tuple(acts),
                          has_small=w_small is not None, w_transposed=w_transposed),
        grid=(m // tm, n // tn),
        in_specs=in_specs,
        out_specs=out_specs,
        out_shape=out_shape,
        scratch_shapes=[pltpu.VMEM((tm, k), BF16)],
        compiler_params=pltpu.CompilerParams(
            dimension_semantics=("arbitrary", "arbitrary"),
            vmem_limit_bytes=VMEM_LIMIT),
        name="norm_matmul",
    )(*args)


def _gmlp_kernel(u_ref, v_ref, z_ref, lng_ref, lnb_ref, ws_ref, sb_ref, o_ref,
                 vn_ref, *, rows):
    v = v_ref[...].astype(F32)
    mu = jnp.mean(v, axis=-1, keepdims=True)
    vc = v - mu
    var = jnp.mean(vc * vc, axis=-1, keepdims=True)
    vn = vc * lax.rsqrt(var + NORM_EPS) * lng_ref[...] + lnb_ref[...]
    vn_ref[...] = vn.astype(BF16)

    row = lax.broadcasted_iota(jnp.int32, (CHUNK, CHUNK), 0)
    col = lax.broadcasted_iota(jnp.int32, (CHUNK, CHUNK), 1)
    causal = row >= col
    for g in range(GMLP_GROUPS):
        cs = slice(g * LANES, (g + 1) * LANES)
        w_s = jnp.where(causal, ws_ref[g], 0.0).astype(BF16)
        for c in range(rows // CHUNK):
            rs = slice(c * CHUNK, (c + 1) * CHUNK)
            v_mix = jnp.dot(w_s, vn_ref[rs, cs], preferred_element_type=F32)
            v_mix = v_mix + sb_ref[:, cs]
            o_ref[rs, cs] = (u_ref[rs, cs].astype(F32) * v_mix
                             * z_ref[rs, cs].astype(F32)).astype(o_ref.dtype)


def _gmlp(uvz, ln_g, ln_b, spatial_w, spatial_b_exp, *, rows):
    m = uvz.shape[0]
    w = GMLP_WIDTH
    return pl.pallas_call(
        functools.partial(_gmlp_kernel, rows=rows),
        grid=(m // rows,),
        in_specs=[
            pl.BlockSpec((rows, w), lambda i: (i, 0)),
            pl.BlockSpec((rows, w), lambda i: (i, 1)),
            pl.BlockSpec((rows, w), lambda i: (i, 2)),
            pl.BlockSpec((1, w), lambda i: (0, 0)),
            pl.BlockSpec((1, w), lambda i: (0, 0)),
            pl.BlockSpec((GMLP_GROUPS, CHUNK, CHUNK), lambda i: (0, 0, 0)),
            pl.BlockSpec((CHUNK, w), lambda i: (0, 0)),
        ],
        out_specs=pl.BlockSpec((rows, w), lambda i: (i, 0)),
        out_shape=jax.ShapeDtypeStruct((m, w), BF16),
        scratch_shapes=[pltpu.VMEM((rows, w), BF16)],
        compiler_params=pltpu.CompilerParams(
            dimension_semantics=("arbitrary",), vmem_limit_bytes=VMEM_LIMIT),
        name="gmlp",
    )(uvz, uvz, uvz, ln_g.reshape(1, w), ln_b.reshape(1, w), spatial_w,
      spatial_b_exp)


def _ssd_kernel(zb_ref, xbc_ref, dt_ref, cw_ref, cb_ref, dtb_ref, alog_ref,
                dskip_ref, ng_ref, e_ref, o_ref, ext_ref, st_ref, y_ref, *, rows):
    step = pl.program_id(1)

    @pl.when(step == 0)
    def _():
        ext_ref[0:HALO, :] = jnp.zeros((HALO, SSD_CONV_DIM), F32)
        st_ref[...] = jnp.zeros(st_ref.shape, F32)

    @pl.when(step != 0)
    def _():
        ext_ref[0:HALO, :] = ext_ref[rows:rows + HALO, :]

    ext_ref[HALO:HALO + rows, :] = xbc_ref[...]

    ext = ext_ref[...]
    conv = cb_ref[...] + cw_ref[SSD_CONV - 1:SSD_CONV, :] * ext[HALO:]
    for k in range(SSD_CONV - 1):
        shift = SSD_CONV - 1 - k
        conv = conv + cw_ref[k:k + 1, :] * pltpu.roll(ext, shift, axis=0)[HALO:]
    xact_all = _silu(conv)

    dt_all = _softplus(dt_ref[...] + dtb_ref[...])
    adt_all = dt_all * (-LOG2E * jnp.exp(alog_ref[...]))
    row = lax.broadcasted_iota(jnp.int32, (CHUNK, CHUNK), 0)
    col = lax.broadcasted_iota(jnp.int32, (CHUNK, CHUNK), 1)
    causal = row >= col
    tri = jnp.where(causal, 1.0, 0.0).astype(BF16)
    tri3 = jnp.concatenate([tri] * SPLIT_TERMS, axis=1)
    lane = lax.broadcasted_iota(jnp.int32, (CHUNK, SSD_WIDTH), 1)
    low_half = (lane % LANES) < SSD_HEAD_DIM
    copy = lax.broadcasted_iota(jnp.int32, (2 * CHUNK, LANES), 1) // SSD_HEADS
    heads_per_group = SSD_HEADS // SSD_GROUPS

    for ci in range(rows // CHUNK):
        rs = slice(ci * CHUNK, (ci + 1) * CHUNK)
        xact = xact_all[rs]
        xs = xact[:, :SSD_WIDTH]
        dt = dt_all[rs]
        a_cum = jnp.dot(tri3, jnp.concatenate(
            [t.astype(BF16) for t in _split_f32(adt_all[rs])], axis=0),
            preferred_element_type=F32)
        a_cum_t = a_cum.T

        terms = _split_f32(jnp.concatenate([dt, a_cum], axis=0))
        packed = jnp.where(copy == 0, terms[0], jnp.where(copy == 1, terms[1], terms[2]))
        expanded = jnp.dot(packed.astype(BF16), e_ref[...],
                           preferred_element_type=F32)
        dt_exp = expanded[:CHUNK]
        acum_exp = expanded[CHUNK:]
        alast_exp = acum_exp[CHUNK - 1:CHUNK, :]
        exp_a = jnp.exp2(acum_exp)
        decay_to_end = jnp.exp2(alast_exp - acum_exp)
        chunk_decay = jnp.exp2(alast_exp)

        xdt = xs * dt_exp
        xdt_lo = jnp.where(low_half, xdt, 0.0).astype(BF16)
        xdt_hi = jnp.where(low_half, 0.0, xdt).astype(BF16)
        xdte = (xdt * decay_to_end).astype(BF16)

        for g in range(SSD_GROUPS):
            gs = slice(g * SSD_GROUP_WIDTH, (g + 1) * SSD_GROUP_WIDTH)
            b_g = xact[:, SSD_WIDTH + g * SSD_STATE:SSD_WIDTH + (g + 1) * SSD_STATE]
            c_g = xact[:, SSD_WIDTH + SSD_BC_DIM + g * SSD_STATE:
                       SSD_WIDTH + SSD_BC_DIM + (g + 1) * SSD_STATE].astype(BF16)
            b_gt = b_g.T.astype(BF16)
            cb = jnp.dot(c_g, b_gt, preferred_element_type=F32)

            st_old = st_ref[g]
            y_off = jnp.dot(c_g, st_old.astype(BF16), preferred_element_type=F32)
            y_off = y_off * exp_a[:, gs]

            for jp in range(heads_per_group // 2):
                blk = g * (heads_per_group // 2) + jp
                bs = slice(blk * LANES, (blk + 1) * LANES)
                ms = []
                for hh in range(2):
                    h = 2 * blk + hh
                    seg = a_cum[:, h:h + 1] - a_cum_t[h:h + 1, :]
                    decay = jnp.exp2(jnp.where(causal, seg, -jnp.inf))
                    ms.append((cb * decay).astype(BF16))
                lhs = jnp.concatenate(ms, axis=1)
                rhs = jnp.concatenate([xdt_lo[:, bs], xdt_hi[:, bs]], axis=0)
                y_diag = jnp.dot(lhs, rhs, preferred_element_type=F32)
                y_ref[rs, bs] = (y_diag + y_off[:, jp * LANES:(jp + 1) * LANES]
                                 + dskip_ref[:, bs] * xs[:, bs])

            st_ref[g] = (st_old * chunk_decay[:, gs]
                         + jnp.dot(b_gt, xdte[:, gs], preferred_element_type=F32))

    for g in range(SSD_GROUPS):
        gs = slice(g * SSD_GROUP_WIDTH, (g + 1) * SSD_GROUP_WIDTH)
        yz = y_ref[:, gs] * zb_ref[:, gs].astype(F32)
        ms = jnp.mean(yz * yz, axis=-1, keepdims=True)
        o_ref[:, gs] = (yz * lax.rsqrt(ms + NORM_EPS) * ng_ref[:, gs]).astype(o_ref.dtype)


def _ssd(uvz, xbc, dt_raw, conv_w, conv_b, dt_bias_p, a_log_p, d_skip_exp, norm_g,
         expand, *, batch, rows):
    m = uvz.shape[0]
    ns = m // batch // rows
    full = lambda shape: pl.BlockSpec(shape, lambda b, c: (0,) * len(shape))
    return pl.pallas_call(
        functools.partial(_ssd_kernel, rows=rows),
        grid=(batch, ns),
        in_specs=[
            pl.BlockSpec((rows, SSD_WIDTH), lambda b, c: (b * ns + c, 3)),
            pl.BlockSpec((rows, SSD_CONV_DIM), lambda b, c: (b * ns + c, 0)),
            pl.BlockSpec((rows, LANES), lambda b, c: (b * ns + c, 0)),
            full((SSD_CONV, SSD_CONV_DIM)),
            full((1, SSD_CONV_DIM)),
            full((1, LANES)),
            full((1, LANES)),
            full((1, SSD_WIDTH)),
            full((1, SSD_WIDTH)),
            full((LANES, SSD_WIDTH)),
        ],
        out_specs=pl.BlockSpec((rows, SSD_WIDTH), lambda b, c: (b * ns + c, 0)),
        out_shape=jax.ShapeDtypeStruct((m, SSD_WIDTH), BF16),
        scratch_shapes=[
            pltpu.VMEM((rows + HALO, SSD_CONV_DIM), F32),
            pltpu.VMEM((SSD_GROUPS, SSD_STATE, SSD_GROUP_WIDTH), F32),
            pltpu.VMEM((rows, SSD_WIDTH), F32),
        ],
        compiler_params=pltpu.CompilerParams(
            dimension_semantics=("arbitrary", "arbitrary"),
            vmem_limit_bytes=VMEM_LIMIT),
        name="ssd",
    )(uvz, xbc, dt_raw, conv_w, conv_b.reshape(1, -1), dt_bias_p, a_log_p, d_skip_exp,
      norm_g.reshape(1, -1), expand)


def _out_proj2_kernel(ya_ref, yb_ref, wa_ref, wb_ref, x_ref, o_ref):
    acc = jnp.dot(ya_ref[...], wa_ref[...], preferred_element_type=F32)
    acc = acc + jnp.dot(yb_ref[...], wb_ref[...], preferred_element_type=F32)
    o_ref[...] = x_ref[...] + acc


def _out_proj2(ya, yb, w, x, *, tm, tn):
    m, k = ya.shape
    n = w.shape[1]
    assert w.shape[0] == 2 * k
    return pl.pallas_call(
        _out_proj2_kernel,
        grid=(m // tm, n // tn),
        in_specs=[
            pl.BlockSpec((tm, k), lambda i, j: (i, 0)),
            pl.BlockSpec((tm, k), lambda i, j: (i, 0)),
            pl.BlockSpec((k, tn), lambda i, j: (0, j)),
            pl.BlockSpec((k, tn), lambda i, j: (1, j)),
            pl.BlockSpec((tm, tn), lambda i, j: (i, j)),
        ],
        out_specs=pl.BlockSpec((tm, tn), lambda i, j: (i, j)),
        out_shape=jax.ShapeDtypeStruct((m, n), F32),
        compiler_params=pltpu.CompilerParams(
            dimension_semantics=("arbitrary", "arbitrary"),
            vmem_limit_bytes=VMEM_LIMIT),
        name="out_proj_even",
    )(ya, yb, w, w, x)


def _out_proj_norm_kernel(y_ref, w_ref, x_ref, g_ref, o_ref):
    x2 = x_ref[...] + jnp.dot(y_ref[...], w_ref[...], preferred_element_type=F32)
    ms = jnp.mean(x2 * x2, axis=-1, keepdims=True)
    o_ref[...] = x2 * lax.rsqrt(ms + NORM_EPS) * g_ref[...]


def _out_proj_norm(y, w, x, g, *, tm):
    m, k = y.shape
    n = w.shape[1]
    return pl.pallas_call(
        _out_proj_norm_kernel,
        grid=(m // tm,),
        in_specs=[
            pl.BlockSpec((tm, k), lambda i: (i, 0)),
            pl.BlockSpec((k, n), lambda i: (0, 0)),
            pl.BlockSpec((tm, n), lambda i: (i, 0)),
            pl.BlockSpec((1, n), lambda i: (0, 0)),
        ],
        out_specs=pl.BlockSpec((tm, n), lambda i: (i, 0)),
        out_shape=jax.ShapeDtypeStruct((m, n), F32),
        compiler_params=pltpu.CompilerParams(
            dimension_semantics=("arbitrary",), vmem_limit_bytes=VMEM_LIMIT),
        name="out_proj_odd_norm",
    )(y, w, x, g.reshape(1, n))


def _diff_attn_kernel(slopes_ref, q_ref, k_ref, v_ref, gate_ref, lq1_ref, lk1_ref,
                      lq2_ref, lk2_ref, sg_ref, o_ref, ka_ref, kb_ref, va_ref, *, tq, hps):
    hp = pl.program_id(1)

    @pl.when((pl.program_id(0) == 0) & (hp == 0))
    def _():
        for hh in range(hps):
            va_ref[hh, :, DIFF_V_DIM:2 * DIFF_V_DIM] = jnp.ones((SEQ, DIFF_V_DIM), BF16)

    lane = lax.broadcasted_iota(jnp.int32, (tq, LANES), 1)
    first_map = lane < DIFF_HEAD_DIM
    sub = lane % DIFF_HEAD_DIM
    rows = lax.broadcasted_iota(jnp.int32, (tq, LANES), 0)
    ones_mask = jnp.where(sub < SPLIT_TERMS, 1.0, 0.0)
    row = lax.broadcasted_iota(jnp.int32, (tq, tq), 0)
    col = lax.broadcasted_iota(jnp.int32, (tq, tq), 1)
    keep = row >= col
    nt_dims = (((1,), (1,)), ((), ()))
    lam = (jnp.exp(jnp.sum(lq1_ref[...] * lk1_ref[...], axis=-1, keepdims=True))
           - jnp.exp(jnp.sum(lq2_ref[...] * lk2_ref[...], axis=-1, keepdims=True))
           + LAMBDA_INIT)

    def probs(qq, kk, nk):
        s = lax.dot_general(qq, kk, nt_dims, preferred_element_type=F32)
        s_diag = jnp.where(keep, s[:, nk - tq:], -jnp.inf)
        m = jnp.max(s_diag, axis=-1, keepdims=True)
        if nk == tq:
            return jnp.exp2(s_diag - m).astype(BF16)
        s_past = s[:, :nk - tq]
        m = jnp.maximum(m, jnp.max(s_past, axis=-1, keepdims=True))
        return jnp.concatenate([jnp.exp2(s_past - m), jnp.exp2(s_diag - m)],
                               axis=1).astype(BF16)

    for qi in range(SEQ // tq):
        for hh in range(hps):
            hl = slice(hh * LANES, (hh + 1) * LANES)
            slope2 = slopes_ref[hp * hps + hh] * LOG2E
            qs = slice(qi * tq, (qi + 1) * tq)
            nk = (qi + 1) * tq
            t1, t2, t3 = _split_f32(slope2 * (rows + qs.start).astype(F32))
            tab = jnp.where(sub == 0, t1, jnp.where(sub == 1, t2, jnp.where(sub == 2, t3, 0.0)))
            k = k_ref[qs, hl].astype(F32)
            ka_ref[hh, qs, :] = jnp.where(first_map, k, tab).astype(BF16)
            kb_ref[hh, qs, :] = jnp.where(first_map, tab, k).astype(BF16)
            va_ref[hh, qs, 0:DIFF_V_DIM] = v_ref[qs, hl]

            qf = q_ref[qs, hl].astype(F32)
            qa = jnp.where(first_map, qf, ones_mask).astype(BF16)
            qb = jnp.where(first_map, ones_mask, qf).astype(BF16)
            p = jnp.concatenate([probs(qa, ka_ref[hh, 0:nk, :], nk),
                                 probs(qb, kb_ref[hh, 0:nk, :], nk)], axis=0)
            acc = jnp.dot(p, va_ref[hh, 0:nk, :], preferred_element_type=F32)
            att = acc[:, :DIFF_V_DIM] / acc[:, DIFF_V_DIM:DIFF_V_DIM + 1]
            o = att[:tq] - lam * att[tq:]
            ms = jnp.mean(o * o, axis=-1, keepdims=True)
            o = o * lax.rsqrt(ms + NORM_EPS) * sg_ref[...]
            o = o * (1.0 - LAMBDA_INIT)
            o_ref[qs, hl] = (o * gate_ref[qs, hl].astype(F32)).astype(o_ref.dtype)


def _diff_attn(qkvg, slopes, lq1, lk1, lq2, lk2, subln_g, *, batch, tq, hps):
    m = qkvg.shape[0]
    w = hps * LANES
    nb = DIFF_HEADS // hps
    vec = lambda n: pl.BlockSpec((1, n), lambda b, h, s: (0, 0))
    grid_spec = pltpu.PrefetchScalarGridSpec(
        num_scalar_prefetch=1,
        grid=(batch, nb),
        in_specs=[
            pl.BlockSpec((SEQ, w), lambda b, h, s: (b, h)),
            pl.BlockSpec((SEQ, w), lambda b, h, s: (b, nb + h)),
            pl.BlockSpec((SEQ, w), lambda b, h, s: (b, 2 * nb + h)),
            pl.BlockSpec((SEQ, w), lambda b, h, s: (b, 3 * nb + h)),
            vec(DIFF_HEAD_DIM), vec(DIFF_HEAD_DIM), vec(DIFF_HEAD_DIM),
            vec(DIFF_HEAD_DIM), vec(DIFF_V_DIM),
        ],
        out_specs=pl.BlockSpec((SEQ, w), lambda b, h, s: (b, h)),
        scratch_shapes=[
            pltpu.VMEM((hps, SEQ, LANES), BF16),
            pltpu.VMEM((hps, SEQ, LANES), BF16),
            pltpu.VMEM((hps, SEQ, 2 * DIFF_V_DIM), BF16),
        ],
    )
    return pl.pallas_call(
        functools.partial(_diff_attn_kernel, tq=tq, hps=hps),
        grid_spec=grid_spec,
        out_shape=jax.ShapeDtypeStruct((m, DIFF_WIDTH), BF16),
        compiler_params=pltpu.CompilerParams(
            dimension_semantics=("arbitrary", "arbitrary"),
            vmem_limit_bytes=VMEM_LIMIT),
        name="diff_attn",
    )(slopes, qkvg, qkvg, qkvg, qkvg, lq1.reshape(1, -1), lk1.reshape(1, -1),
      lq2.reshape(1, -1), lk2.reshape(1, -1), subln_g.reshape(1, -1))


def kernel(x, l0_norm_g, l0_w_in, l0_gmlp_ln_g, l0_gmlp_ln_b, l0_spatial_w,
           l0_spatial_b, l0_conv_w, l0_conv_b, l0_dt_bias, l0_a_log, l0_d_skip,
           l0_ssm_norm_g, l0_w_out, l1_norm_g, l1_w_in, l1_lambda_q1,
           l1_lambda_k1, l1_lambda_q2, l1_lambda_k2, l1_subln_g, l1_w_out,
           final_norm_g):
    batch, seq, d = x.shape
    m = batch * seq
    x2d = x.reshape(m, d)

    def head_lanes(v):
        v = jnp.concatenate([v] * SPLIT_TERMS, axis=0)
        return jnp.pad(v, [(0, LANES - SPLIT_TERMS * SSD_HEADS)] + [(0, 0)] * (v.ndim - 1))

    w_in0_t = l0_w_in.T.astype(BF16)
    w_dt_t = head_lanes(w_in0_t[EVEN_MAIN:])
    uvz, xbc, dt_raw = _norm_matmul(
        x2d, l0_norm_g, w_in0_t,
        [(BF16, [(2 * GMLP_WIDTH, "gelu"), (GMLP_WIDTH + SSD_WIDTH, "silu")]),
         (F32, [(SSD_CONV_DIM, None)])],
        tm=1024, tn=1024, w_small=w_dt_t, w_transposed=True)

    spatial_b_exp = jnp.repeat(l0_spatial_b.T, LANES, axis=1)
    y_a = _gmlp(uvz, l0_gmlp_ln_g, l0_gmlp_ln_b, l0_spatial_w, spatial_b_exp,
                rows=512)

    dt_bias_p = head_lanes(l0_dt_bias).reshape(1, LANES)
    a_log_p = head_lanes(l0_a_log).reshape(1, LANES)
    d_skip_exp = jnp.repeat(l0_d_skip, SSD_HEAD_DIM).reshape(1, SSD_WIDTH)
    lane_id = jnp.arange(LANES)
    expand = ((lane_id[:, None] % SSD_HEADS
               == (jnp.arange(SSD_WIDTH) // SSD_HEAD_DIM)[None, :])
              & (lane_id[:, None] < SPLIT_TERMS * SSD_HEADS)).astype(BF16)
    y_b = _ssd(uvz, xbc, dt_raw, l0_conv_w, l0_conv_b, dt_bias_p, a_log_p, d_skip_exp,
               l0_ssm_norm_g, expand, batch=batch, rows=256)

    x1 = _out_proj2(y_a, y_b, l0_w_out.astype(BF16), x2d, tm=1024, tn=1024)

    q_scale = LOG2E * DIFF_HEAD_DIM ** -0.5
    (qkvg,) = _norm_matmul(
        x1, l1_norm_g, l1_w_in.astype(BF16),
        [(BF16, [(DIFF_WIDTH, ("scale", q_scale)), (2 * DIFF_WIDTH, None),
                 (DIFF_WIDTH, "silu")])], tm=1024, tn=2048)
    slopes = 2.0 ** (-8.0 * (jnp.arange(DIFF_HEADS, dtype=F32) + 1.0) / DIFF_HEADS)
    o = _diff_attn(qkvg, slopes, l1_lambda_q1, l1_lambda_k1, l1_lambda_q2,
                   l1_lambda_k2, l1_subln_g, batch=batch, tq=256, hps=4)

    out = _out_proj_norm(o, l1_w_out.astype(BF16), x1, final_norm_g, tm=512)
    return out.reshape(batch, seq, d)
```

```python
import functools
import math

import jax
import jax.numpy as jnp
from jax import lax
from jax.experimental import pallas as pl
from jax.experimental.pallas import tpu as pltpu

F32 = jnp.float32
BF16 = jnp.bfloat16

D_MODEL = 2048
SEQ = 2048
CHUNK = 128
NORM_EPS = 1e-5

GMLP_WIDTH = 2048
GMLP_GROUPS = 16
SSD_WIDTH = 2048
SSD_HEADS = 32
SSD_HEAD_DIM = 64
SSD_GROUPS = 4
SSD_STATE = 128
SSD_CONV = 4
SSD_BC_DIM = SSD_GROUPS * SSD_STATE
SSD_CONV_DIM = SSD_WIDTH + 2 * SSD_BC_DIM
SSD_GROUP_WIDTH = SSD_WIDTH // SSD_GROUPS
UVZ_WIDTH = 3 * GMLP_WIDTH + SSD_WIDTH
EVEN_MAIN = UVZ_WIDTH + SSD_CONV_DIM

DIFF_HEADS = 16
DIFF_HEAD_DIM = 64
DIFF_V_DIM = 128
DIFF_WIDTH = DIFF_HEADS * DIFF_V_DIM
LAMBDA_INIT = 0.8 - 0.6 * math.exp(-0.3 * 1)
LOG2E = math.log2(math.e)

LANES = 128
HALO = 8
VMEM_LIMIT = 56 * 1024 * 1024

PROJ_TM = 1024
EVEN_PROJ_TN = 1024
ODD_PROJ_TN = 2048
OUT_EVEN_TN = 1024
OUT_ODD_TM = 512
GMLP_ROWS = 1024
SSD_ROWS = 512
ATTN_TQ = 256
ATTN_HEADS_PER_STEP = 4


def _silu(x):
    half = 0.5 * x
    return half + half * jnp.tanh(half)


SPLIT_TERMS = 3


def _split_f32(x):
    t1 = x.astype(BF16).astype(F32)
    r1 = x - t1
    t2 = r1.astype(BF16).astype(F32)
    t3 = (r1 - t2).astype(BF16).astype(F32)
    return t1, t2, t3


def _gelu_tanh(x):
    c = math.sqrt(2.0 / math.pi)
    return 0.5 * x * (1.0 + jnp.tanh(c * (x + 0.044715 * (x * x * x))))


def _softplus(x):
    return jnp.maximum(x, 0.0) + jnp.log1p(jnp.exp(-jnp.abs(x)))


def _norm_matmul_kernel(*refs, bounds, acts, has_small, w_transposed):
    dims = (((1,), (1,)), ((), ())) if w_transposed else (((1,), (0,)), ((), ()))
    refs = list(refs)
    x_ref, g_ref, w_ref = refs[:3]
    del refs[:3]
    ws_ref = refs.pop(0) if has_small else None
    out_refs = [refs.pop(0) for _ in bounds]
    small_ref = refs.pop(0) if has_small else None
    h_ref = refs.pop(0)

    j = pl.program_id(1)

    @pl.when(j == 0)
    def _():
        xf = x_ref[...]
        ms = jnp.mean(xf * xf, axis=-1, keepdims=True)
        h = (xf * lax.rsqrt(ms + NORM_EPS) * g_ref[...]).astype(BF16)
        h_ref[...] = h
        if has_small:
            small_ref[...] = lax.dot_general(h, ws_ref[...], dims,
                                             preferred_element_type=F32)

    branches = []
    for (lo, _), seg_acts, o_ref in zip(bounds, acts, out_refs):
        for nblk, act in seg_acts:
            branches.append((lo, lo + nblk, act, o_ref))
            lo += nblk
    for lo, hi, act, o_ref in branches:
        @pl.when((j >= lo) & (j < hi))
        def _(act=act, o_ref=o_ref):
            acc = lax.dot_general(h_ref[...], w_ref[...], dims, preferred_element_type=F32)
            if act is None:
                val = acc
            elif act == "gelu":
                val = _gelu_tanh(acc)
            elif act == "silu":
                val = _silu(acc)
            else:
                kind, c = act
                assert kind == "scale"
                val = acc * c
            o_ref[...] = val.astype(o_ref.dtype)


def _norm_matmul(x, g, w, segs, *, tm, tn, w_small=None, w_transposed=False):
    m, k = x.shape
    assert m % tm == 0
    bounds = []
    acts = []
    lo = 0
    for _, ranges in segs:
        start = lo
        for ncols, act in ranges:
            assert ncols % tn == 0
            lo += ncols
        bounds.append((start // tn, lo // tn))
        acts.append(tuple((ncols // tn, act) for ncols, act in ranges))
    n = lo
    n_axis = 0 if w_transposed else 1
    assert n <= w.shape[n_axis] and w.shape[1 - n_axis] == k

    if w_transposed:
        w_spec = pl.BlockSpec((tn, k), lambda i, j: (j, 0))
    else:
        w_spec = pl.BlockSpec((k, tn), lambda i, j: (0, j))
    in_specs = [
        pl.BlockSpec((tm, k), lambda i, j: (i, 0)),
        pl.BlockSpec((1, k), lambda i, j: (0, 0)),
        w_spec,
    ]
    args = [x, g.reshape(1, k), w]
    if w_small is not None:
        in_specs.append(pl.BlockSpec(w_small.shape, lambda i, j: (0, 0)))
        args.append(w_small)

    out_specs = []
    out_shape = []
    for (blo, bhi), (dtype, _) in zip(bounds, segs):
        def idx(i, j, blo=blo, bhi=bhi):
            return (i, jnp.clip(j - blo, 0, bhi - blo - 1))
        out_specs.append(pl.BlockSpec((tm, tn), idx))
        out_shape.append(jax.ShapeDtypeStruct((m, (bhi - blo) * tn), dtype))
    if w_small is not None:
        ns = w_small.shape[n_axis]
        out_specs.append(pl.BlockSpec((tm, ns), lambda i, j: (i, 0)))
        out_shape.append(jax.ShapeDtypeStruct((m, ns), F32))

    return pl.pallas_call(
        functools.partial(_norm_matmul_kernel, bounds=tuple(bounds),
                          acts=tuple(acts),
                          has_small=w_small is not None, w_transposed=w_transposed),
        grid=(m // tm, n // tn),
        in_specs=in_specs,
        out_specs=out_specs,
        out_shape=out_shape,
        scratch_shapes=[pltpu.VMEM((tm, k), BF16)],
        compiler_params=pltpu.CompilerParams(
            dimension_semantics=("arbitrary", "arbitrary"),
            vmem_limit_bytes=VMEM_LIMIT),
        name="norm_matmul",
    )(*args)


def _gmlp_kernel(u_ref, v_ref, z_ref, lng_ref, lnb_ref, ws_ref, sb_ref, o_ref,
                 vn_ref, *, rows):
    v = v_ref[...].astype(F32)
    mu = jnp.mean(v, axis=-1, keepdims=True)
    vc = v - mu
    var = jnp.mean(vc * vc, axis=-1, keepdims=True)
    vn = vc * lax.rsqrt(var + NORM_EPS) * lng_ref[...] + lnb_ref[...]
    vn_ref[...] = vn.astype(BF16)

    row = lax.broadcasted_iota(jnp.int32, (CHUNK, CHUNK), 0)
    col = lax.broadcasted_iota(jnp.int32, (CHUNK, CHUNK), 1)
    causal = row >= col
    for g in range(GMLP_GROUPS):
        cs = slice(g * LANES, (g + 1) * LANES)
        w_s = jnp.where(causal, ws_ref[g], 0.0).astype(BF16)
        for c in range(rows // CHUNK):
            rs = slice(c * CHUNK, (c + 1) * CHUNK)
            v_mix = jnp.dot(w_s, vn_ref[rs, cs], preferred_element_type=F32)
            v_mix = v_mix + sb_ref[:, cs]
            o_ref[rs, cs] = (u_ref[rs, cs].astype(F32) * v_mix
                             * z_ref[rs, cs].astype(F32)).astype(o_ref.dtype)


def _gmlp(uvz, ln_g, ln_b, spatial_w, spatial_b_exp, *, rows):
    m = uvz.shape[0]
    w = GMLP_WIDTH
    return pl.pallas_call(
        functools.partial(_gmlp_kernel, rows=rows),
        grid=(m // rows,),
        in_specs=[
            pl.BlockSpec((rows, w), lambda i: (i, 0)),
            pl.BlockSpec((rows, w), lambda i: (i, 1)),
            pl.BlockSpec((rows, w), lambda i: (i, 2)),
            pl.BlockSpec((1, w), lambda i: (0, 0)),
            pl.BlockSpec((1, w), lambda i: (0, 0)),
            pl.BlockSpec((GMLP_GROUPS, CHUNK, CHUNK), lambda i: (0, 0, 0)),
            pl.BlockSpec((CHUNK, w), lambda i: (0, 0)),
        ],
        out_specs=pl.BlockSpec((rows, w), lambda i: (i, 0)),
        out_shape=jax.ShapeDtypeStruct((m, w), BF16),
        scratch_shapes=[pltpu.VMEM((rows, w), BF16)],
        compiler_params=pltpu.CompilerParams(
            dimension_semantics=("arbitrary",), vmem_limit_bytes=VMEM_LIMIT),
        name="gmlp",
    )(uvz, uvz, uvz, ln_g.reshape(1, w), ln_b.reshape(1, w), spatial_w,
      spatial_b_exp)


def _ssd_kernel(zb_ref, xbc_ref, dt_ref, cw_ref, cb_ref, dtb_ref, alog_ref,
                dskip_ref, ng_ref, e_ref, o_ref, ext_ref, st_ref, y_ref, *, rows):
    step = pl.program_id(1)

    @pl.when(step == 0)
    def _():
        ext_ref[0:HALO, :] = jnp.zeros((HALO, SSD_CONV_DIM), F32)
        st_ref[...] = jnp.zeros(st_ref.shape, F32)

    @pl.when(step != 0)
    def _():
        ext_ref[0:HALO, :] = ext_ref[rows:rows + HALO, :]

    ext_ref[HALO:HALO + rows, :] = xbc_ref[...]

    ext = ext_ref[...]
    conv = cb_ref[...] + cw_ref[SSD_CONV - 1:SSD_CONV, :] * ext[HALO:]
    for k in range(SSD_CONV - 1):
        shift = SSD_CONV - 1 - k
        conv = conv + cw_ref[k:k + 1, :] * pltpu.roll(ext, shift, axis=0)[HALO:]
    xact_all = _silu(conv)

    dt_all = _softplus(dt_ref[...] + dtb_ref[...])
    adt_all = dt_all * (-LOG2E * jnp.exp(alog_ref[...]))
    row = lax.broadcasted_iota(jnp.int32, (CHUNK, CHUNK), 0)
    col = lax.broadcasted_iota(jnp.int32, (CHUNK, CHUNK), 1)
    causal = row >= col
    tri = jnp.where(causal, 1.0, 0.0).astype(BF16)
    tri3 = jnp.concatenate([tri] * SPLIT_TERMS, axis=1)
    lane = lax.broadcasted_iota(jnp.int32, (CHUNK, SSD_WIDTH), 1)
    low_half = (lane % LANES) < SSD_HEAD_DIM
    copy = lax.broadcasted_iota(jnp.int32, (2 * CHUNK, LANES), 1) // SSD_HEADS
    heads_per_group = SSD_HEADS // SSD_GROUPS

    for ci in range(rows // CHUNK):
        rs = slice(ci * CHUNK, (ci + 1) * CHUNK)
        xact = xact_all[rs]
        xs = xact[:, :SSD_WIDTH]
        dt = dt_all[rs]
        a_cum = jnp.dot(tri3, jnp.concatenate(
            [t.astype(BF16) for t in _split_f32(adt_all[rs])], axis=0),
            preferred_element_type=F32)
        a_cum_t = a_cum.T

        terms = _split_f32(jnp.concatenate([dt, a_cum], axis=0))
        packed = jnp.where(copy == 0, terms[0], jnp.where(copy == 1, terms[1], terms[2]))
        expanded = jnp.dot(packed.astype(BF16), e_ref[...],
                           preferred_element_type=F32)
        dt_exp = expanded[:CHUNK]
        acum_exp = expanded[CHUNK:]
        alast_exp = acum_exp[CHUNK - 1:CHUNK, :]
        exp_a = jnp.exp2(acum_exp)
        decay_to_end = jnp.exp2(alast_exp - acum_exp)
        chunk_decay = jnp.exp2(alast_exp)

        xdt = xs * dt_exp
        xdt_lo = jnp.where(low_half, xdt, 0.0).astype(BF16)
        xdt_hi = jnp.where(low_half, 0.0, xdt).astype(BF16)
        xdte = (xdt * decay_to_end).astype(BF16)

        for g in range(SSD_GROUPS):
            gs = slice(g * SSD_GROUP_WIDTH, (g + 1) * SSD_GROUP_WIDTH)
            b_g = xact[:, SSD_WIDTH + g * SSD_STATE:SSD_WIDTH + (g + 1) * SSD_STATE]
            c_g = xact[:, SSD_WIDTH + SSD_BC_DIM + g * SSD_STATE:
                       SSD_WIDTH + SSD_BC_DIM + (g + 1) * SSD_STATE].astype(BF16)
            b_gt = b_g.T.astype(BF16)
            cb = jnp.dot(c_g, b_gt, preferred_element_type=F32)

            st_old = st_ref[g]
            y_off = jnp.dot(c_g, st_old.astype(BF16), preferred_element_type=F32)
            y_off = y_off * exp_a[:, gs]

            for jp in range(heads_per_group // 2):
                blk = g * (heads_per_group // 2) + jp
                bs = slice(blk * LANES, (blk + 1) * LANES)
                ms = []
                for hh in range(2):
                    h = 2 * blk + hh
                    seg = a_cum[:, h:h + 1] - a_cum_t[h:h + 1, :]
                    decay = jnp.exp2(jnp.where(causal, seg, -jnp.inf))
                    ms.append((cb * decay).astype(BF16))
                lhs = jnp.concatenate(ms, axis=1)
                rhs = jnp.concatenate([xdt_lo[:, bs], xdt_hi[:, bs]], axis=0)
                y_diag = jnp.dot(lhs, rhs, preferred_element_type=F32)
                y_ref[rs, bs] = (y_diag + y_off[:, jp * LANES:(jp + 1) * LANES]
                                 + dskip_ref[:, bs] * xs[:, bs])

            st_ref[g] = (st_old * chunk_decay[:, gs]
                         + jnp.dot(b_gt, xdte[:, gs], preferred_element_type=F32))

    for g in range(SSD_GROUPS):
        gs = slice(g * SSD_GROUP_WIDTH, (g + 1) * SSD_GROUP_WIDTH)
        yz = y_ref[:, gs] * zb_ref[:, gs].astype(F32)
        ms = jnp.mean(yz * yz, axis=-1, keepdims=True)
        o_ref[:, gs] = (yz * lax.rsqrt(ms + NORM_EPS) * ng_ref[:, gs]).astype(o_ref.dtype)


def _ssd(uvz, xbc, dt_raw, conv_w, conv_b, dt_bias_p, a_log_p, d_skip_exp, norm_g,
         expand, *, batch, rows):
    m = uvz.shape[0]
    ns = m // batch // rows
    full = lambda shape: pl.BlockSpec(shape, lambda b, c: (0,) * len(shape))
    return pl.pallas_call(
        functools.partial(_ssd_kernel, rows=rows),
        grid=(batch, ns),
        in_specs=[
            pl.BlockSpec((rows, SSD_WIDTH), lambda b, c: (b * ns + c, 3)),
            pl.BlockSpec((rows, SSD_CONV_DIM), lambda b, c: (b * ns + c, 0)),
            pl.BlockSpec((rows, LANES), lambda b, c: (b * ns + c, 0)),
            full((SSD_CONV, SSD_CONV_DIM)),
            full((1, SSD_CONV_DIM)),
            full((1, LANES)),
            full((1, LANES)),
            full((1, SSD_WIDTH)),
            full((1, SSD_WIDTH)),
            full((LANES, SSD_WIDTH)),
        ],
        out_specs=pl.BlockSpec((rows, SSD_WIDTH), lambda b, c: (b * ns + c, 0)),
        out_shape=jax.ShapeDtypeStruct((m, SSD_WIDTH), BF16),
        scratch_shapes=[
            pltpu.VMEM((rows + HALO, SSD_CONV_DIM), F32),
            pltpu.VMEM((SSD_GROUPS, SSD_STATE, SSD_GROUP_WIDTH), F32),
            pltpu.VMEM((rows, SSD_WIDTH), F32),
        ],
        compiler_params=pltpu.CompilerParams(
            dimension_semantics=("arbitrary", "arbitrary"),
            vmem_limit_bytes=VMEM_LIMIT),
        name="ssd",
    )(uvz, xbc, dt_raw, conv_w, conv_b.reshape(1, -1), dt_bias_p, a_log_p, d_skip_exp,
      norm_g.reshape(1, -1), expand)


def _out_proj2_kernel(ya_ref, yb_ref, wa_ref, wb_ref, x_ref, o_ref):
    acc = jnp.dot(ya_ref[...], wa_ref[...], preferred_element_type=F32)
    acc = acc + jnp.dot(yb_ref[...], wb_ref[...], preferred_element_type=F32)
    o_ref[...] = x_ref[...] + acc


def _out_proj2(ya, yb, w, x, *, tm, tn):
    m, k = ya.shape
    n = w.shape[1]
    assert w.shape[0] == 2 * k
    return pl.pallas_call(
        _out_proj2_kernel,
        grid=(m // tm, n // tn),
        in_specs=[
            pl.BlockSpec((tm, k), lambda i, j: (i, 0)),
            pl.BlockSpec((tm, k), lambda i, j: (i, 0)),
            pl.BlockSpec((k, tn), lambda i, j: (0, j)),
            pl.BlockSpec((k, tn), lambda i, j: (1, j)),
            pl.BlockSpec((tm, tn), lambda i, j: (i, j)),
        ],
        out_specs=pl.BlockSpec((tm, tn), lambda i, j: (i, j)),
        out_shape=jax.ShapeDtypeStruct((m, n), F32),
        compiler_params=pltpu.CompilerParams(
            dimension_semantics=("arbitrary", "arbitrary"),
            vmem_limit_bytes=VMEM_LIMIT),
        name="out_proj_even",
    )(ya, yb, w, w, x)


def _out_proj_norm_kernel(y_ref, w_ref, x_ref, g_ref, o_ref):
    x2 = x_ref[...] + jnp.dot(y_ref[...], w_ref[...], preferred_element_type=F32)
    ms = jnp.mean(x2 * x2, axis=-1, keepdims=True)
    o_ref[...] = x2 * lax.rsqrt(ms + NORM_EPS) * g_ref[...]


def _out_proj_norm(y, w, x, g, *, tm):
    m, k = y.shape
    n = w.shape[1]
    return pl.pallas_call(
        _out_proj_norm_kernel,
        grid=(m // tm,),
        in_specs=[
            pl.BlockSpec((tm, k), lambda i: (i, 0)),
            pl.BlockSpec((k, n), lambda i: (0, 0)),
            pl.BlockSpec((tm, n), lambda i: (i, 0)),
            pl.BlockSpec((1, n), lambda i: (0, 0)),
        ],
        out_specs=pl.BlockSpec((tm, n), lambda i: (i, 0)),
        out_shape=jax.ShapeDtypeStruct((m, n), F32),
        compiler_params=pltpu.CompilerParams(
            dimension_semantics=("arbitrary",), vmem_limit_bytes=VMEM_LIMIT),
        name="out_proj_odd_norm",
    )(y, w, x, g.reshape(1, n))


def _diff_attn_kernel(slopes_ref, q_ref, k_ref, v_ref, gate_ref, lq1_ref, lk1_ref,
                      lq2_ref, lk2_ref, sg_ref, o_ref, ka_ref, kb_ref, va_ref, *, tq, hps):
    hp = pl.program_id(1)

    @pl.when((pl.program_id(0) == 0) & (hp == 0))
    def _():
        for hh in range(hps):
            va_ref[hh, :, DIFF_V_DIM:2 * DIFF_V_DIM] = jnp.ones((SEQ, DIFF_V_DIM), BF16)

    lane = lax.broadcasted_iota(jnp.int32, (tq, LANES), 1)
    first_map = lane < DIFF_HEAD_DIM
    sub = lane % DIFF_HEAD_DIM
    rows = lax.broadcasted_iota(jnp.int32, (tq, LANES), 0)
    ones_mask = jnp.where(sub < SPLIT_TERMS, 1.0, 0.0)
    row = lax.broadcasted_iota(jnp.int32, (tq, tq), 0)
    col = lax.broadcasted_iota(jnp.int32, (tq, tq), 1)
    keep = row >= col
    nt_dims = (((1,), (1,)), ((), ()))
    lam = (jnp.exp(jnp.sum(lq1_ref[...] * lk1_ref[...], axis=-1, keepdims=True))
           - jnp.exp(jnp.sum(lq2_ref[...] * lk2_ref[...], axis=-1, keepdims=True))
           + LAMBDA_INIT)

    def probs(qq, kk, nk):
        s = lax.dot_general(qq, kk, nt_dims, preferred_element_type=F32)
        s_diag = jnp.where(keep, s[:, nk - tq:], -jnp.inf)
        m = jnp.max(s_diag, axis=-1, keepdims=True)
        if nk == tq:
            return jnp.exp2(s_diag - m).astype(BF16)
        s_past = s[:, :nk - tq]
        m = jnp.maximum(m, jnp.max(s_past, axis=-1, keepdims=True))
        return jnp.concatenate([jnp.exp2(s_past - m), jnp.exp2(s_diag - m)],
                               axis=1).astype(BF16)

    for qi in range(SEQ // tq):
        for hh in range(hps):
            hl = slice(hh * LANES, (hh + 1) * LANES)
            slope2 = slopes_ref[hp * hps + hh] * LOG2E
            qs = slice(qi * tq, (qi + 1) * tq)
            nk = (qi + 1) * tq
            t1, t2, t3 = _split_f32(slope2 * (rows + qs.start).astype(F32))
            tab = jnp.where(sub == 0, t1, jnp.where(sub == 1, t2, jnp.where(sub == 2, t3, 0.0)))
            k = k_ref[qs, hl].astype(F32)
            ka_ref[hh, qs, :] = jnp.where(first_map, k, tab).astype(BF16)
            kb_ref[hh, qs, :] = jnp.where(first_map, tab, k).astype(BF16)
            va_ref[hh, qs, 0:DIFF_V_DIM] = v_ref[qs, hl]

            qf = q_ref[qs, hl].astype(F32)
            qa = jnp.where(first_map, qf, ones_mask).astype(BF16)
            qb = jnp.where(first_map, ones_mask, qf).astype(BF16)
            p = jnp.concatenate([probs(qa, ka_ref[hh, 0:nk, :], nk),
                                 probs(qb, kb_ref[hh, 0:nk, :], nk)], axis=0)
            acc = jnp.dot(p, va_ref[hh, 0:nk, :], preferred_element_type=F32)
            att = acc[:, :DIFF_V_DIM] / acc[:, DIFF_V_DIM:DIFF_V_DIM + 1]
            o = att[:tq] - lam * att[tq:]
            ms = jnp.mean(o * o, axis=-1, keepdims=True)
            o = o * lax.rsqrt(ms + NORM_EPS) * sg_ref[...]
            o = o * (1.0 - LAMBDA_INIT)
            o_ref[qs, hl] = (o * gate_ref[qs, hl].astype(F32)).astype(o_ref.dtype)


def _diff_attn(qkvg, slopes, lq1, lk1, lq2, lk2, subln_g, *, batch, tq, hps):
    m = qkvg.shape[0]
    w = hps * LANES
    nb = DIFF_HEADS // hps
    vec = lambda n: pl.BlockSpec((1, n), lambda b, h, s: (0, 0))
    grid_spec = pltpu.PrefetchScalarGridSpec(
        num_scalar_prefetch=1,
        grid=(batch, nb),
        in_specs=[
            pl.BlockSpec((SEQ, w), lambda b, h, s: (b, h)),
            pl.BlockSpec((SEQ, w), lambda b, h, s: (b, nb + h)),
            pl.BlockSpec((SEQ, w), lambda b, h, s: (b, 2 * nb + h)),
            pl.BlockSpec((SEQ, w), lambda b, h, s: (b, 3 * nb + h)),
            vec(DIFF_HEAD_DIM), vec(DIFF_HEAD_DIM), vec(DIFF_HEAD_DIM),
            vec(DIFF_HEAD_DIM), vec(DIFF_V_DIM),
        ],
        out_specs=pl.BlockSpec((SEQ, w), lambda b, h, s: (b, h)),
        scratch_shapes=[
            pltpu.VMEM((hps, SEQ, LANES), BF16),
            pltpu.VMEM((hps, SEQ, LANES), BF16),
            pltpu.VMEM((hps, SEQ, 2 * DIFF_V_DIM), BF16),
        ],
    )
    return pl.pallas_call(
        functools.partial(_diff_attn_kernel, tq=tq, hps=hps),
        grid_spec=grid_spec,
        out_shape=jax.ShapeDtypeStruct((m, DIFF_WIDTH), BF16),
        compiler_params=pltpu.CompilerParams(
            dimension_semantics=("arbitrary", "arbitrary"),
            vmem_limit_bytes=VMEM_LIMIT),
        name="diff_attn",
    )(slopes, qkvg, qkvg, qkvg, qkvg, lq1.reshape(1, -1), lk1.reshape(1, -1),
      lq2.reshape(1, -1), lk2.reshape(1, -1), subln_g.reshape(1, -1))


def kernel(x, l0_norm_g, l0_w_in, l0_gmlp_ln_g, l0_gmlp_ln_b, l0_spatial_w,
           l0_spatial_b, l0_conv_w, l0_conv_b, l0_dt_bias, l0_a_log, l0_d_skip,
           l0_ssm_norm_g, l0_w_out, l1_norm_g, l1_w_in, l1_lambda_q1,
           l1_lambda_k1, l1_lambda_q2, l1_lambda_k2, l1_subln_g, l1_w_out,
           final_norm_g):
    batch, seq, d = x.shape
    m = batch * seq
    x2d = x.reshape(m, d)

    def head_lanes(v):
        v = jnp.concatenate([v] * SPLIT_TERMS, axis=0)
        return jnp.pad(v, [(0, LANES - SPLIT_TERMS * SSD_HEADS)] + [(0, 0)] * (v.ndim - 1))

    w_in0_t = l0_w_in.T.astype(BF16)
    w_dt_t = head_lanes(w_in0_t[EVEN_MAIN:])
    uvz, xbc, dt_raw = _norm_matmul(
        x2d, l0_norm_g, w_in0_t,
        [(BF16, [(2 * GMLP_WIDTH, "gelu"), (GMLP_WIDTH + SSD_WIDTH, "silu")]),
         (F32, [(SSD_CONV_DIM, None)])],
        tm=PROJ_TM, tn=EVEN_PROJ_TN, w_small=w_dt_t, w_transposed=True)

    spatial_b_exp = jnp.repeat(l0_spatial_b.T, LANES, axis=1)
    y_a = _gmlp(uvz, l0_gmlp_ln_g, l0_gmlp_ln_b, l0_spatial_w, spatial_b_exp,
                rows=GMLP_ROWS)

    dt_bias_p = head_lanes(l0_dt_bias).reshape(1, LANES)
    a_log_p = head_lanes(l0_a_log).reshape(1, LANES)
    d_skip_exp = jnp.repeat(l0_d_skip, SSD_HEAD_DIM).reshape(1, SSD_WIDTH)
    lane_id = jnp.arange(LANES)
    expand = ((lane_id[:, None] % SSD_HEADS
               == (jnp.arange(SSD_WIDTH) // SSD_HEAD_DIM)[None, :])
              & (lane_id[:, None] < SPLIT_TERMS * SSD_HEADS)).astype(BF16)
    y_b = _ssd(uvz, xbc, dt_raw, l0_conv_w, l0_conv_b, dt_bias_p, a_log_p, d_skip_exp,
               l0_ssm_norm_g, expand, batch=batch, rows=SSD_ROWS)

    x1 = _out_proj2(y_a, y_b, l0_w_out.astype(BF16), x2d, tm=PROJ_TM,
                    tn=OUT_EVEN_TN)

    q_scale = LOG2E * DIFF_HEAD_DIM ** -0.5
    (qkvg,) = _norm_matmul(
        x1, l1_norm_g, l1_w_in.astype(BF16),
        [(BF16, [(DIFF_WIDTH, ("scale", q_scale)), (2 * DIFF_WIDTH, None),
                 (DIFF_WIDTH, "silu")])], tm=PROJ_TM, tn=ODD_PROJ_TN)
    slopes = 2.0 ** (-8.0 * (jnp.arange(DIFF_HEADS, dtype=F32) + 1.0) / DIFF_HEADS)
    o = _diff_attn(qkvg, slopes, l1_lambda_q1, l1_lambda_k1, l1_lambda_q2,
                   l1_lambda_k2, l1_subln_g, batch=batch, tq=ATTN_TQ,
                   hps=ATTN_HEADS_PER_STEP)

    out = _out_proj_norm(o, l1_w_out.astype(BF16), x1, final_norm_g, tm=OUT_ODD_TM)
    return out.reshape(batch, seq, d)
```

```python
import functools
import math

import jax
import jax.numpy as jnp
from jax import lax
from jax.experimental import pallas as pl
from jax.experimental.pallas import tpu as pltpu

F32 = jnp.float32
BF16 = jnp.bfloat16

D_MODEL = 2048
SEQ = 2048
CHUNK = 128
NORM_EPS = 1e-5

GMLP_WIDTH = 2048
GMLP_GROUPS = 16
SSD_WIDTH = 2048
SSD_HEADS = 32
SSD_HEAD_DIM = 64
SSD_GROUPS = 4
SSD_STATE = 128
SSD_CONV = 4
SSD_BC_DIM = SSD_GROUPS * SSD_STATE
SSD_CONV_DIM = SSD_WIDTH + 2 * SSD_BC_DIM
SSD_GROUP_WIDTH = SSD_WIDTH // SSD_GROUPS
UVZ_WIDTH = 3 * GMLP_WIDTH + SSD_WIDTH
EVEN_MAIN = UVZ_WIDTH + SSD_CONV_DIM

DIFF_HEADS = 16
DIFF_HEAD_DIM = 64
DIFF_V_DIM = 128
DIFF_WIDTH = DIFF_HEADS * DIFF_V_DIM
LAMBDA_INIT = 0.8 - 0.6 * math.exp(-0.3 * 1)
LOG2E = math.log2(math.e)

LANES = 128
HALO = 8
VMEM_LIMIT = 56 * 1024 * 1024

PROJ_TM = 1024
EVEN_PROJ_TN = 1024
ODD_PROJ_TN = 2048
OUT_EVEN_TN = 1024
OUT_ODD_TM = 512
GMLP_ROWS = 1024
SSD_ROWS = 512
ATTN_TQ = 256
ATTN_HEADS_PER_STEP = 4


def _silu(x):
    half = 0.5 * x
    return half + half * jnp.tanh(half)


SPLIT_TERMS = 3


def _split_f32(x):
    t1 = x.astype(BF16).astype(F32)
    r1 = x - t1
    t2 = r1.astype(BF16).astype(F32)
    t3 = (r1 - t2).astype(BF16).astype(F32)
    return t1, t2, t3


def _gelu_tanh(x):
    c = math.sqrt(2.0 / math.pi)
    return 0.5 * x * (1.0 + jnp.tanh(c * (x + 0.044715 * (x * x * x))))


def _softplus(x):
    return jnp.maximum(x, 0.0) + jnp.log1p(jnp.exp(-jnp.abs(x)))


def _norm_matmul_kernel(*refs, bounds, acts, has_small, w_transposed):
    dims = (((1,), (1,)), ((), ())) if w_transposed else (((1,), (0,)), ((), ()))
    refs = list(refs)
    x_ref, g_ref, w_ref = refs[:3]
    del refs[:3]
    ws_ref = refs.pop(0) if has_small else None
    out_refs = [refs.pop(0) for _ in bounds]
    small_ref = refs.pop(0) if has_small else None
    h_ref = refs.pop(0)

    j = pl.program_id(1)

    @pl.when(j == 0)
    def _():
        xf = x_ref[...]
        ms = jnp.mean(xf * xf, axis=-1, keepdims=True)
        h = (xf * lax.rsqrt(ms + NORM_EPS) * g_ref[...]).astype(BF16)
        h_ref[...] = h
        if has_small:
            small_ref[...] = lax.dot_general(h, ws_ref[...], dims,
                                             preferred_element_type=F32)

    branches = []
    for (lo, _), seg_acts, o_ref in zip(bounds, acts, out_refs):
        for nblk, act in seg_acts:
            branches.append((lo, lo + nblk, act, o_ref))
            lo += nblk
    for lo, hi, act, o_ref in branches:
        @pl.when((j >= lo) & (j < hi))
        def _(act=act, o_ref=o_ref):
            acc = lax.dot_general(h_ref[...], w_ref[...], dims, preferred_element_type=F32)
            if act is None:
                val = acc
            elif act == "gelu":
                val = _gelu_tanh(acc)
            elif act == "silu":
                val = _silu(acc)
            else:
                kind, c = act
                assert kind == "scale"
                val = acc * c
            o_ref[...] = val.astype(o_ref.dtype)


def _norm_matmul(x, g, w, segs, *, tm, tn, w_small=None, w_transposed=False):
    m, k = x.shape
    assert m % tm == 0
    bounds = []
    acts = []
    lo = 0
    for _, ranges in segs:
        start = lo
        for ncols, act in ranges:
            assert ncols % tn == 0
            lo += ncols
        bounds.append((start // tn, lo // tn))
        acts.append(tuple((ncols // tn, act) for ncols, act in ranges))
    n = lo
    n_axis = 0 if w_transposed else 1
    assert n <= w.shape[n_axis] and w.shape[1 - n_axis] == k

    if w_transposed:
        w_spec = pl.BlockSpec((tn, k), lambda i, j: (j, 0))
    else:
        w_spec = pl.BlockSpec((k, tn), lambda i, j: (0, j))
    in_specs = [
        pl.BlockSpec((tm, k), lambda i, j: (i, 0)),
        pl.BlockSpec((1, k), lambda i, j: (0, 0)),
        w_spec,
    ]
    args = [x, g.reshape(1, k), w]
    if w_small is not None:
        in_specs.append(pl.BlockSpec(w_small.shape, lambda i, j: (0, 0)))
        args.append(w_small)

    out_specs = []
    out_shape = []
    for (blo, bhi), (dtype, _) in zip(bounds, segs):
        def idx(i, j, blo=blo, bhi=bhi):
            return (i, jnp.clip(j - blo, 0, bhi - blo - 1))
        out_specs.append(pl.BlockSpec((tm, tn), idx))
        out_shape.append(jax.ShapeDtypeStruct((m, (bhi - blo) * tn), dtype))
    if w_small is not None:
        ns = w_small.shape[n_axis]
        out_specs.append(pl.BlockSpec((tm, ns), lambda i, j: (i, 0)))
        out_shape.append(jax.ShapeDtypeStruct((m, ns), F32))

    return pl.pallas_call(
        functools.partial(_norm_matmul_kernel, bounds=tuple(bounds),
                          acts=tuple(acts),
                          has_small=w_small is not None, w_transposed=w_transposed),
        grid=(m // tm, n // tn),
        in_specs=in_specs,
        out_specs=out_specs,
        out_shape=out_shape,
        scratch_shapes=[pltpu.VMEM((tm, k), BF16)],
        compiler_params=pltpu.CompilerParams(
            dimension_semantics=("arbitrary", "arbitrary"),
            vmem_limit_bytes=VMEM_LIMIT),
        name="norm_matmul",
    )(*args)


def _gmlp_kernel(u_ref, v_ref, z_ref, lng_ref, lnb_ref, ws_ref, sb_ref, o_ref,
                 vn_ref, *, rows):
    v = v_ref[...].astype(F32)
    mu = jnp.mean(v, axis=-1, keepdims=True)
    vc = v - mu
    var = jnp.mean(vc * vc, axis=-1, keepdims=True)
    vn = vc * lax.rsqrt(var + NORM_EPS) * lng_ref[...] + lnb_ref[...]
    vn_ref[...] = vn.astype(BF16)

    row = lax.broadcasted_iota(jnp.int32, (CHUNK, CHUNK), 0)
    col = lax.broadcasted_iota(jnp.int32, (CHUNK, CHUNK), 1)
    causal = row >= col
    for g in range(GMLP_GROUPS):
        cs = slice(g * LANES, (g + 1) * LANES)
        w_s = jnp.where(causal, ws_ref[g], 0.0).astype(BF16)
        for c in range(rows // CHUNK):
            rs = slice(c * CHUNK, (c + 1) * CHUNK)
            v_mix = jnp.dot(w_s, vn_ref[rs, cs], preferred_element_type=F32)
            v_mix = v_mix + sb_ref[:, cs]
            o_ref[rs, cs] = (u_ref[rs, cs].astype(F32) * v_mix
                             * z_ref[rs, cs].astype(F32)).astype(o_ref.dtype)


def _gmlp(uvz, ln_g, ln_b, spatial_w, spatial_b_exp, *, rows):
    m = uvz.shape[0]
    w = GMLP_WIDTH
    return pl.pallas_call(
        functools.partial(_gmlp_kernel, rows=rows),
        grid=(m // rows,),
        in_specs=[
            pl.BlockSpec((rows, w), lambda i: (i, 0)),
            pl.BlockSpec((rows, w), lambda i: (i, 1)),
            pl.BlockSpec((rows, w), lambda i: (i, 2)),
            pl.BlockSpec((1, w), lambda i: (0, 0)),
            pl.BlockSpec((1, w), lambda i: (0, 0)),
            pl.BlockSpec((GMLP_GROUPS, CHUNK, CHUNK), lambda i: (0, 0, 0)),
            pl.BlockSpec((CHUNK, w), lambda i: (0, 0)),
        ],
        out_specs=pl.BlockSpec((rows, w), lambda i: (i, 0)),
        out_shape=jax.ShapeDtypeStruct((m, w), BF16),
        scratch_shapes=[pltpu.VMEM((rows, w), BF16)],
        compiler_params=pltpu.CompilerParams(
            dimension_semantics=("arbitrary",), vmem_limit_bytes=VMEM_LIMIT),
        name="gmlp",
    )(uvz, uvz, uvz, ln_g.reshape(1, w), ln_b.reshape(1, w), spatial_w,
      spatial_b_exp)


def _ssd_kernel(zb_ref, xbc_ref, dt_ref, cw_ref, cb_ref, dtb_ref, alog_ref,
                dskip_ref, ng_ref, e_ref, *rest, rows, n_casts):
    cast_in = rest[:n_casts]
    o_ref = rest[n_casts]
    cast_out = rest[n_casts + 1:2 * n_casts + 1]
    ext_ref, st_ref, y_ref = rest[2 * n_casts + 1:]
    step = pl.program_id(1)

    for wi_ref, wo_ref in zip(cast_in, cast_out):
        wo_ref[...] = wi_ref[...].astype(wo_ref.dtype)

    @pl.when(step == 0)
    def _():
        ext_ref[0:HALO, :] = jnp.zeros((HALO, SSD_CONV_DIM), F32)
        st_ref[...] = jnp.zeros(st_ref.shape, F32)

    @pl.when(step != 0)
    def _():
        ext_ref[0:HALO, :] = ext_ref[rows:rows + HALO, :]

    ext_ref[HALO:HALO + rows, :] = xbc_ref[...]

    ext = ext_ref[...]
    conv = cb_ref[...] + cw_ref[SSD_CONV - 1:SSD_CONV, :] * ext[HALO:]
    for k in range(SSD_CONV - 1):
        shift = SSD_CONV - 1 - k
        conv = conv + cw_ref[k:k + 1, :] * pltpu.roll(ext, shift, axis=0)[HALO:]
    xact_all = _silu(conv)

    dt_all = _softplus(dt_ref[...] + dtb_ref[...])
    adt_all = dt_all * (-LOG2E * jnp.exp(alog_ref[...]))
    row = lax.broadcasted_iota(jnp.int32, (CHUNK, CHUNK), 0)
    col = lax.broadcasted_iota(jnp.int32, (CHUNK, CHUNK), 1)
    causal = row >= col
    tri = jnp.where(causal, 1.0, 0.0).astype(BF16)
    tri3 = jnp.concatenate([tri] * SPLIT_TERMS, axis=1)
    lane = lax.broadcasted_iota(jnp.int32, (CHUNK, SSD_WIDTH), 1)
    low_half = (lane % LANES) < SSD_HEAD_DIM
    copy = lax.broadcasted_iota(jnp.int32, (2 * CHUNK, LANES), 1) // SSD_HEADS
    heads_per_group = SSD_HEADS // SSD_GROUPS

    for ci in range(rows // CHUNK):
        rs = slice(ci * CHUNK, (ci + 1) * CHUNK)
        xact = xact_all[rs]
        xs = xact[:, :SSD_WIDTH]
        dt = dt_all[rs]
        a_cum = jnp.dot(tri3, jnp.concatenate(
            [t.astype(BF16) for t in _split_f32(adt_all[rs])], axis=0),
            preferred_element_type=F32)
        a_cum_t = a_cum.T

        terms = _split_f32(jnp.concatenate([dt, a_cum], axis=0))
        packed = jnp.where(copy == 0, terms[0], jnp.where(copy == 1, terms[1], terms[2]))
        expanded = jnp.dot(packed.astype(BF16), e_ref[...],
                           preferred_element_type=F32)
        dt_exp = expanded[:CHUNK]
        acum_exp = expanded[CHUNK:]
        alast_exp = acum_exp[CHUNK - 1:CHUNK, :]
        exp_a = jnp.exp2(acum_exp)
        decay_to_end = jnp.exp2(alast_exp - acum_exp)
        chunk_decay = jnp.exp2(alast_exp)

        xdt = xs * dt_exp
        xdt_lo = jnp.where(low_half, xdt, 0.0).astype(BF16)
        xdt_hi = jnp.where(low_half, 0.0, xdt).astype(BF16)
        xdte = (xdt * decay_to_end).astype(BF16)

        for g in range(SSD_GROUPS):
            gs = slice(g * SSD_GROUP_WIDTH, (g + 1) * SSD_GROUP_WIDTH)
            b_g = xact[:, SSD_WIDTH + g * SSD_STATE:SSD_WIDTH + (g + 1) * SSD_STATE]
            c_g = xact[:, SSD_WIDTH + SSD_BC_DIM + g * SSD_STATE:
                       SSD_WIDTH + SSD_BC_DIM + (g + 1) * SSD_STATE].astype(BF16)
            b_gt = b_g.T.astype(BF16)
            cb = jnp.dot(c_g, b_gt, preferred_element_type=F32)

            st_old = st_ref[g]
            y_off = jnp.dot(c_g, st_old.astype(BF16), preferred_element_type=F32)
            y_off = y_off * exp_a[:, gs]

            for jp in range(heads_per_group // 2):
                blk = g * (heads_per_group // 2) + jp
                bs = slice(blk * LANES, (blk + 1) * LANES)
                ms = []
                for hh in range(2):
                    h = 2 * blk + hh
                    seg = a_cum[:, h:h + 1] - a_cum_t[h:h + 1, :]
                    decay = jnp.exp2(jnp.where(causal, seg, -jnp.inf))
                    ms.append((cb * decay).astype(BF16))
                lhs = jnp.concatenate(ms, axis=1)
                rhs = jnp.concatenate([xdt_lo[:, bs], xdt_hi[:, bs]], axis=0)
                y_diag = jnp.dot(lhs, rhs, preferred_element_type=F32)
                y_ref[rs, bs] = (y_diag + y_off[:, jp * LANES:(jp + 1) * LANES]
                                 + dskip_ref[:, bs] * xs[:, bs])

            st_ref[g] = (st_old * chunk_decay[:, gs]
                         + jnp.dot(b_gt, xdte[:, gs], preferred_element_type=F32))

    for g in range(SSD_GROUPS):
        gs = slice(g * SSD_GROUP_WIDTH, (g + 1) * SSD_GROUP_WIDTH)
        yz = y_ref[:, gs] * zb_ref[:, gs].astype(F32)
        ms = jnp.mean(yz * yz, axis=-1, keepdims=True)
        o_ref[:, gs] = (yz * lax.rsqrt(ms + NORM_EPS) * ng_ref[:, gs]).astype(o_ref.dtype)


def _ssd(uvz, xbc, dt_raw, conv_w, conv_b, dt_bias_p, a_log_p, d_skip_exp, norm_g,
         expand, casts, *, batch, rows):
    m = uvz.shape[0]
    ns = m // batch // rows
    full = lambda shape: pl.BlockSpec(shape, lambda b, c: (0,) * len(shape))
    cast_specs = []
    for w in casts:
        assert w.shape[0] % (batch * ns * 16) == 0
        cast_specs.append(pl.BlockSpec((w.shape[0] // (batch * ns), w.shape[1]),
                                       lambda b, c: (b * ns + c, 0)))
    return pl.pallas_call(
        functools.partial(_ssd_kernel, rows=rows, n_casts=len(casts)),
        grid=(batch, ns),
        in_specs=[
            pl.BlockSpec((rows, SSD_WIDTH), lambda b, c: (b * ns + c, 3)),
            pl.BlockSpec((rows, SSD_CONV_DIM), lambda b, c: (b * ns + c, 0)),
            pl.BlockSpec((rows, LANES), lambda b, c: (b * ns + c, 0)),
            full((SSD_CONV, SSD_CONV_DIM)),
            full((1, SSD_CONV_DIM)),
            full((1, LANES)),
            full((1, LANES)),
            full((1, SSD_WIDTH)),
            full((1, SSD_WIDTH)),
            full((LANES, SSD_WIDTH)),
        ] + cast_specs,
        out_specs=[pl.BlockSpec((rows, SSD_WIDTH), lambda b, c: (b * ns + c, 0))]
        + cast_specs,
        out_shape=[jax.ShapeDtypeStruct((m, SSD_WIDTH), BF16)]
        + [jax.ShapeDtypeStruct(w.shape, BF16) for w in casts],
        scratch_shapes=[
            pltpu.VMEM((rows + HALO, SSD_CONV_DIM), F32),
            pltpu.VMEM((SSD_GROUPS, SSD_STATE, SSD_GROUP_WIDTH), F32),
            pltpu.VMEM((rows, SSD_WIDTH), F32),
        ],
        compiler_params=pltpu.CompilerParams(
            dimension_semantics=("arbitrary", "arbitrary"),
            vmem_limit_bytes=VMEM_LIMIT),
        name="ssd",
    )(uvz, xbc, dt_raw, conv_w, conv_b.reshape(1, -1), dt_bias_p, a_log_p, d_skip_exp,
      norm_g.reshape(1, -1), expand, *casts)


def _out_proj2_kernel(ya_ref, yb_ref, wa_ref, wb_ref, x_ref, o_ref):
    acc = jnp.dot(ya_ref[...], wa_ref[...], preferred_element_type=F32)
    acc = acc + jnp.dot(yb_ref[...], wb_ref[...], preferred_element_type=F32)
    o_ref[...] = x_ref[...] + acc


def _out_proj2(ya, yb, w, x, *, tm, tn):
    m, k = ya.shape
    n = w.shape[1]
    assert w.shape[0] == 2 * k
    return pl.pallas_call(
        _out_proj2_kernel,
        grid=(m // tm, n // tn),
        in_specs=[
            pl.BlockSpec((tm, k), lambda i, j: (i, 0)),
            pl.BlockSpec((tm, k), lambda i, j: (i, 0)),
            pl.BlockSpec((k, tn), lambda i, j: (0, j)),
            pl.BlockSpec((k, tn), lambda i, j: (1, j)),
            pl.BlockSpec((tm, tn), lambda i, j: (i, j)),
        ],
        out_specs=pl.BlockSpec((tm, tn), lambda i, j: (i, j)),
        out_shape=jax.ShapeDtypeStruct((m, n), F32),
        compiler_params=pltpu.CompilerParams(
            dimension_semantics=("arbitrary", "arbitrary"),
            vmem_limit_bytes=VMEM_LIMIT),
        name="out_proj_even",
    )(ya, yb, w, w, x)


def _out_proj_norm_kernel(y_ref, w_ref, x_ref, g_ref, o_ref):
    x2 = x_ref[...] + jnp.dot(y_ref[...], w_ref[...], preferred_element_type=F32)
    ms = jnp.mean(x2 * x2, axis=-1, keepdims=True)
    o_ref[...] = x2 * lax.rsqrt(ms + NORM_EPS) * g_ref[...]


def _out_proj_norm(y, w, x, g, *, tm):
    m, k = y.shape
    n = w.shape[1]
    return pl.pallas_call(
        _out_proj_norm_kernel,
        grid=(m // tm,),
        in_specs=[
            pl.BlockSpec((tm, k), lambda i: (i, 0)),
            pl.BlockSpec((k, n), lambda i: (0, 0)),
            pl.BlockSpec((tm, n), lambda i: (i, 0)),
            pl.BlockSpec((1, n), lambda i: (0, 0)),
        ],
        out_specs=pl.BlockSpec((tm, n), lambda i: (i, 0)),
        out_shape=jax.ShapeDtypeStruct((m, n), F32),
        compiler_params=pltpu.CompilerParams(
            dimension_semantics=("arbitrary",), vmem_limit_bytes=VMEM_LIMIT),
        name="out_proj_odd_norm",
    )(y, w, x, g.reshape(1, n))


def _diff_attn_kernel(slopes_ref, q_ref, k_ref, v_ref, gate_ref, lq1_ref, lk1_ref,
                      lq2_ref, lk2_ref, sg_ref, o_ref, ka_ref, kb_ref, va_ref, *, tq, hps):
    hp = pl.program_id(1)

    @pl.when((pl.program_id(0) == 0) & (hp == 0))
    def _():
        for hh in range(hps):
            va_ref[hh, :, DIFF_V_DIM:2 * DIFF_V_DIM] = jnp.ones((SEQ, DIFF_V_DIM), BF16)

    lane = lax.broadcasted_iota(jnp.int32, (tq, LANES), 1)
    first_map = lane < DIFF_HEAD_DIM
    sub = lane % DIFF_HEAD_DIM
    rows = lax.broadcasted_iota(jnp.int32, (tq, LANES), 0)
    ones_mask = jnp.where(sub < SPLIT_TERMS, 1.0, 0.0)
    row = lax.broadcasted_iota(jnp.int32, (tq, tq), 0)
    col = lax.broadcasted_iota(jnp.int32, (tq, tq), 1)
    keep = row >= col
    nt_dims = (((1,), (1,)), ((), ()))
    lam = (jnp.exp(jnp.sum(lq1_ref[...] * lk1_ref[...], axis=-1, keepdims=True))
           - jnp.exp(jnp.sum(lq2_ref[...] * lk2_ref[...], axis=-1, keepdims=True))
           + LAMBDA_INIT)

    def probs(qq, kk, nk):
        s = lax.dot_general(qq, kk, nt_dims, preferred_element_type=F32)
        s_diag = jnp.where(keep, s[:, nk - tq:], -jnp.inf)
        m = jnp.max(s_diag, axis=-1, keepdims=True)
        if nk == tq:
            return jnp.exp2(s_diag - m).astype(BF16)
        s_past = s[:, :nk - tq]
        m = jnp.maximum(m, jnp.max(s_past, axis=-1, keepdims=True))
        return jnp.concatenate([jnp.exp2(s_past - m), jnp.exp2(s_diag - m)],
                               axis=1).astype(BF16)

    for qi in range(SEQ // tq):
        for hh in range(hps):
            hl = slice(hh * LANES, (hh + 1) * LANES)
            slope2 = slopes_ref[hp * hps + hh] * LOG2E
            qs = slice(qi * tq, (qi + 1) * tq)
            nk = (qi + 1) * tq
            t1, t2, t3 = _split_f32(slope2 * (rows + qs.start).astype(F32))
            tab = jnp.where(sub == 0, t1, jnp.where(sub == 1, t2, jnp.where(sub == 2, t3, 0.0)))
            k = k_ref[qs, hl].astype(F32)
            ka_ref[hh, qs, :] = jnp.where(first_map, k, tab).astype(BF16)
            kb_ref[hh, qs, :] = jnp.where(first_map, tab, k).astype(BF16)
            va_ref[hh, qs, 0:DIFF_V_DIM] = v_ref[qs, hl]

            qf = q_ref[qs, hl].astype(F32)
            qa = jnp.where(first_map, qf, ones_mask).astype(BF16)
            qb = jnp.where(first_map, ones_mask, qf).astype(BF16)
            p = jnp.concatenate([probs(qa, ka_ref[hh, 0:nk, :], nk),
                                 probs(qb, kb_ref[hh, 0:nk, :], nk)], axis=0)
            acc = jnp.dot(p, va_ref[hh, 0:nk, :], preferred_element_type=F32)
            att = acc[:, :DIFF_V_DIM] / acc[:, DIFF_V_DIM:DIFF_V_DIM + 1]
            o = att[:tq] - lam * att[tq:]
            ms = jnp.mean(o * o, axis=-1, keepdims=True)
            o = o * lax.rsqrt(ms + NORM_EPS) * sg_ref[...]
            o = o * (1.0 - LAMBDA_INIT)
            o_ref[qs, hl] = (o * gate_ref[qs, hl].astype(F32)).astype(o_ref.dtype)


def _diff_attn(qkvg, slopes, lq1, lk1, lq2, lk2, subln_g, *, batch, tq, hps):
    m = qkvg.shape[0]
    w = hps * LANES
    nb = DIFF_HEADS // hps
    vec = lambda n: pl.BlockSpec((1, n), lambda b, h, s: (0, 0))
    grid_spec = pltpu.PrefetchScalarGridSpec(
        num_scalar_prefetch=1,
        grid=(batch, nb),
        in_specs=[
            pl.BlockSpec((SEQ, w), lambda b, h, s: (b, h)),
            pl.BlockSpec((SEQ, w), lambda b, h, s: (b, nb + h)),
            pl.BlockSpec((SEQ, w), lambda b, h, s: (b, 2 * nb + h)),
            pl.BlockSpec((SEQ, w), lambda b, h, s: (b, 3 * nb + h)),
            vec(DIFF_HEAD_DIM), vec(DIFF_HEAD_DIM), vec(DIFF_HEAD_DIM),
            vec(DIFF_HEAD_DIM), vec(DIFF_V_DIM),
        ],
        out_specs=pl.BlockSpec((SEQ, w), lambda b, h, s: (b, h)),
        scratch_shapes=[
            pltpu.VMEM((hps, SEQ, LANES), BF16),
            pltpu.VMEM((hps, SEQ, LANES), BF16),
            pltpu.VMEM((hps, SEQ, 2 * DIFF_V_DIM), BF16),
        ],
    )
    return pl.pallas_call(
        functools.partial(_diff_attn_kernel, tq=tq, hps=hps),
        grid_spec=grid_spec,
        out_shape=jax.ShapeDtypeStruct((m, DIFF_WIDTH), BF16),
        compiler_params=pltpu.CompilerParams(
            dimension_semantics=("arbitrary", "arbitrary"),
            vmem_limit_bytes=VMEM_LIMIT),
        name="diff_attn",
    )(slopes, qkvg, qkvg, qkvg, qkvg, lq1.reshape(1, -1), lk1.reshape(1, -1),
      lq2.reshape(1, -1), lk2.reshape(1, -1), subln_g.reshape(1, -1))


def kernel(x, l0_norm_g, l0_w_in, l0_gmlp_ln_g, l0_gmlp_ln_b, l0_spatial_w,
           l0_spatial_b, l0_conv_w, l0_conv_b, l0_dt_bias, l0_a_log, l0_d_skip,
           l0_ssm_norm_g, l0_w_out, l1_norm_g, l1_w_in, l1_lambda_q1,
           l1_lambda_k1, l1_lambda_q2, l1_lambda_k2, l1_subln_g, l1_w_out,
           final_norm_g):
    batch, seq, d = x.shape
    m = batch * seq
    x2d = x.reshape(m, d)

    def head_lanes(v):
        v = jnp.concatenate([v] * SPLIT_TERMS, axis=0)
        return jnp.pad(v, [(0, LANES - SPLIT_TERMS * SSD_HEADS)] + [(0, 0)] * (v.ndim - 1))

    w_in0_t = l0_w_in.T.astype(BF16)
    w_dt_t = head_lanes(w_in0_t[EVEN_MAIN:])
    uvz, xbc, dt_raw = _norm_matmul(
        x2d, l0_norm_g, w_in0_t,
        [(BF16, [(2 * GMLP_WIDTH, "gelu"), (GMLP_WIDTH + SSD_WIDTH, "silu")]),
         (F32, [(SSD_CONV_DIM, None)])],
        tm=PROJ_TM, tn=EVEN_PROJ_TN, w_small=w_dt_t, w_transposed=True)

    spatial_b_exp = jnp.repeat(l0_spatial_b.T, LANES, axis=1)
    y_a = _gmlp(uvz, l0_gmlp_ln_g, l0_gmlp_ln_b, l0_spatial_w, spatial_b_exp,
                rows=GMLP_ROWS)

    dt_bias_p = head_lanes(l0_dt_bias).reshape(1, LANES)
    a_log_p = head_lanes(l0_a_log).reshape(1, LANES)
    d_skip_exp = jnp.repeat(l0_d_skip, SSD_HEAD_DIM).reshape(1, SSD_WIDTH)
    lane_id = jnp.arange(LANES)
    expand = ((lane_id[:, None] % SSD_HEADS
               == (jnp.arange(SSD_WIDTH) // SSD_HEAD_DIM)[None, :])
              & (lane_id[:, None] < SPLIT_TERMS * SSD_HEADS)).astype(BF16)
    y_b, w_out0, w_in1, w_out1 = _ssd(
        uvz, xbc, dt_raw, l0_conv_w, l0_conv_b, dt_bias_p, a_log_p, d_skip_exp,
        l0_ssm_norm_g, expand, [l0_w_out, l1_w_in, l1_w_out], batch=batch, rows=SSD_ROWS)

    x1 = _out_proj2(y_a, y_b, w_out0, x2d, tm=PROJ_TM, tn=OUT_EVEN_TN)

    q_scale = LOG2E * DIFF_HEAD_DIM ** -0.5
    (qkvg,) = _norm_matmul(
        x1, l1_norm_g, w_in1,
        [(BF16, [(DIFF_WIDTH, ("scale", q_scale)), (2 * DIFF_WIDTH, None),
                 (DIFF_WIDTH, "silu")])], tm=PROJ_TM, tn=ODD_PROJ_TN)
    slopes = 2.0 ** (-8.0 * (jnp.arange(DIFF_HEADS, dtype=F32) + 1.0) / DIFF_HEADS)
    o = _diff_attn(qkvg, slopes, l1_lambda_q1, l1_lambda_k1, l1_lambda_q2,
                   l1_lambda_k2, l1_subln_g, batch=batch, tq=ATTN_TQ,
                   hps=ATTN_HEADS_PER_STEP)

    out = _out_proj_norm(o, w_out1, x1, final_norm_g, tm=OUT_ODD_TM)
    return out.reshape(batch, seq, d)
```

```python
import functools
import math

import jax
import jax.numpy as jnp
from jax import lax
from jax.experimental import pallas as pl
from jax.experimental.pallas import tpu as pltpu

F32 = jnp.float32
BF16 = jnp.bfloat16

D_MODEL = 2048
SEQ = 2048
CHUNK = 128
NORM_EPS = 1e-5

GMLP_WIDTH = 2048
GMLP_GROUPS = 16
SSD_WIDTH = 2048
SSD_HEADS = 32
SSD_HEAD_DIM = 64
SSD_GROUPS = 4
SSD_STATE = 128
SSD_CONV = 4
SSD_BC_DIM = SSD_GROUPS * SSD_STATE
SSD_CONV_DIM = SSD_WIDTH + 2 * SSD_BC_DIM
SSD_GROUP_WIDTH = SSD_WIDTH // SSD_GROUPS
UVZ_WIDTH = 3 * GMLP_WIDTH + SSD_WIDTH
EVEN_MAIN = UVZ_WIDTH + SSD_CONV_DIM

DIFF_HEADS = 16
DIFF_HEAD_DIM = 64
DIFF_V_DIM = 128
DIFF_WIDTH = DIFF_HEADS * DIFF_V_DIM
LAMBDA_INIT = 0.8 - 0.6 * math.exp(-0.3 * 1)
LOG2E = math.log2(math.e)

LANES = 128
HALO = 8
VMEM_LIMIT = 56 * 1024 * 1024

PROJ_TM = 1024
EVEN_PROJ_TN = 1024
ODD_PROJ_TN = 2048
OUT_EVEN_TN = 1024
OUT_ODD_TM = 512
GMLP_ROWS = 1024
SSD_ROWS = 512
ATTN_TQ = 256
ATTN_HEADS_PER_STEP = 4


def _silu(x):
    half = 0.5 * x
    return half + half * jnp.tanh(half)


SPLIT_TERMS = 3


def _split_f32(x):
    t1 = x.astype(BF16).astype(F32)
    r1 = x - t1
    t2 = r1.astype(BF16).astype(F32)
    t3 = (r1 - t2).astype(BF16).astype(F32)
    return t1, t2, t3


def _gelu_tanh(x):
    c = math.sqrt(2.0 / math.pi)
    return 0.5 * x * (1.0 + jnp.tanh(c * (x + 0.044715 * (x * x * x))))


def _softplus(x):
    return jnp.maximum(x, 0.0) + jnp.log1p(jnp.exp(-jnp.abs(x)))


def _norm_matmul_kernel(*refs, bounds, acts, has_small, w_transposed):
    dims = (((1,), (1,)), ((), ())) if w_transposed else (((1,), (0,)), ((), ()))
    refs = list(refs)
    x_ref, g_ref, w_ref = refs[:3]
    del refs[:3]
    ws_ref = refs.pop(0) if has_small else None
    out_refs = [refs.pop(0) for _ in bounds]
    small_ref = refs.pop(0) if has_small else None
    h_ref = refs.pop(0)

    j = pl.program_id(1)

    @pl.when(j == 0)
    def _():
        xf = x_ref[...]
        ms = jnp.mean(xf * xf, axis=-1, keepdims=True)
        h = (xf * lax.rsqrt(ms + NORM_EPS) * g_ref[...]).astype(BF16)
        h_ref[...] = h
        if has_small:
            small_ref[...] = lax.dot_general(h, ws_ref[...], dims,
                                             preferred_element_type=F32)

    branches = []
    for (lo, _), seg_acts, o_ref in zip(bounds, acts, out_refs):
        for nblk, act in seg_acts:
            branches.append((lo, lo + nblk, act, o_ref))
            lo += nblk
    for lo, hi, act, o_ref in branches:
        @pl.when((j >= lo) & (j < hi))
        def _(act=act, o_ref=o_ref):
            acc = lax.dot_general(h_ref[...], w_ref[...], dims, preferred_element_type=F32)
            if act is None:
                val = acc
            elif act == "gelu":
                val = _gelu_tanh(acc)
            elif act == "silu":
                val = _silu(acc)
            else:
                kind, c = act
                assert kind == "scale"
                val = acc * c
            o_ref[...] = val.astype(o_ref.dtype)


def _norm_matmul(x, g, w, segs, *, tm, tn, w_small=None, w_transposed=False):
    m, k = x.shape
    assert m % tm == 0
    bounds = []
    acts = []
    lo = 0
    for _, ranges in segs:
        start = lo
        for ncols, act in ranges:
            assert ncols % tn == 0
            lo += ncols
        bounds.append((start // tn, lo // tn))
        acts.append(tuple((ncols // tn, act) for ncols, act in ranges))
    n = lo
    n_axis = 0 if w_transposed else 1
    assert n <= w.shape[n_axis] and w.shape[1 - n_axis] == k

    if w_transposed:
        w_spec = pl.BlockSpec((tn, k), lambda i, j: (j, 0))
    else:
        w_spec = pl.BlockSpec((k, tn), lambda i, j: (0, j))
    in_specs = [
        pl.BlockSpec((tm, k), lambda i, j: (i, 0)),
        pl.BlockSpec((1, k), lambda i, j: (0, 0)),
        w_spec,
    ]
    args = [x, g.reshape(1, k), w]
    if w_small is not None:
        in_specs.append(pl.BlockSpec(w_small.shape, lambda i, j: (0, 0)))
        args.append(w_small)

    out_specs = []
    out_shape = []
    for (blo, bhi), (dtype, _) in zip(bounds, segs):
        def idx(i, j, blo=blo, bhi=bhi):
            return (i, jnp.clip(j - blo, 0, bhi - blo - 1))
        out_specs.append(pl.BlockSpec((tm, tn), idx))
        out_shape.append(jax.ShapeDtypeStruct((m, (bhi - blo) * tn), dtype))
    if w_small is not None:
        ns = w_small.shape[n_axis]
        out_specs.append(pl.BlockSpec((tm, ns), lambda i, j: (i, 0)))
        out_shape.append(jax.ShapeDtypeStruct((m, ns), F32))

    return pl.pallas_call(
        functools.partial(_norm_matmul_kernel, bounds=tuple(bounds),
                          acts=tuple(acts),
                          has_small=w_small is not None, w_transposed=w_transposed),
        grid=(m // tm, n // tn),
        in_specs=in_specs,
        out_specs=out_specs,
        out_shape=out_shape,
        scratch_shapes=[pltpu.VMEM((tm, k), BF16)],
        compiler_params=pltpu.CompilerParams(
            dimension_semantics=("arbitrary", "arbitrary"),
            vmem_limit_bytes=VMEM_LIMIT),
        name="norm_matmul",
    )(*args)


def _gmlp_kernel(u_ref, v_ref, z_ref, lng_ref, lnb_ref, ws_ref, sb_ref, o_ref,
                 vn_ref, *, rows):
    v = v_ref[...].astype(F32)
    mu = jnp.mean(v, axis=-1, keepdims=True)
    vc = v - mu
    var = jnp.mean(vc * vc, axis=-1, keepdims=True)
    vn = vc * lax.rsqrt(var + NORM_EPS) * lng_ref[...] + lnb_ref[...]
    vn_ref[...] = vn.astype(BF16)

    row = lax.broadcasted_iota(jnp.int32, (CHUNK, CHUNK), 0)
    col = lax.broadcasted_iota(jnp.int32, (CHUNK, CHUNK), 1)
    causal = row >= col
    for g in range(GMLP_GROUPS):
        cs = slice(g * LANES, (g + 1) * LANES)
        w_s = jnp.where(causal, ws_ref[g], 0.0).astype(BF16)
        for c in range(rows // CHUNK):
            rs = slice(c * CHUNK, (c + 1) * CHUNK)
            v_mix = jnp.dot(w_s, vn_ref[rs, cs], preferred_element_type=F32)
            v_mix = v_mix + sb_ref[:, cs]
            o_ref[rs, cs] = (u_ref[rs, cs].astype(F32) * v_mix
                             * z_ref[rs, cs].astype(F32)).astype(o_ref.dtype)


def _gmlp(uvz, ln_g, ln_b, spatial_w, spatial_b_exp, *, rows):
    m = uvz.shape[0]
    w = GMLP_WIDTH
    return pl.pallas_call(
        functools.partial(_gmlp_kernel, rows=rows),
        grid=(m // rows,),
        in_specs=[
            pl.BlockSpec((rows, w), lambda i: (i, 0)),
            pl.BlockSpec((rows, w), lambda i: (i, 1)),
            pl.BlockSpec((rows, w), lambda i: (i, 2)),
            pl.BlockSpec((1, w), lambda i: (0, 0)),
            pl.BlockSpec((1, w), lambda i: (0, 0)),
            pl.BlockSpec((GMLP_GROUPS, CHUNK, CHUNK), lambda i: (0, 0, 0)),
            pl.BlockSpec((CHUNK, w), lambda i: (0, 0)),
        ],
        out_specs=pl.BlockSpec((rows, w), lambda i: (i, 0)),
        out_shape=jax.ShapeDtypeStruct((m, w), BF16),
        scratch_shapes=[pltpu.VMEM((rows, w), BF16)],
        compiler_params=pltpu.CompilerParams(
            dimension_semantics=("arbitrary",), vmem_limit_bytes=VMEM_LIMIT),
        name="gmlp",
    )(uvz, uvz, uvz, ln_g.reshape(1, w), ln_b.reshape(1, w), spatial_w,
      spatial_b_exp)


def _ssd_kernel(zb_ref, xbc_ref, dt_ref, cw_ref, cb_ref, dtb_ref, alog_ref,
                dskip_ref, ng_ref, e_ref, *rest, rows, n_casts):
    cast_in = rest[:n_casts]
    o_ref = rest[n_casts]
    cast_out = rest[n_casts + 1:2 * n_casts + 1]
    ext_ref, st_ref, y_ref = rest[2 * n_casts + 1:]
    step = pl.program_id(1)

    for wi_ref, wo_ref in zip(cast_in, cast_out):
        wo_ref[...] = wi_ref[...].astype(wo_ref.dtype)

    @pl.when(step == 0)
    def _():
        ext_ref[0:HALO, :] = jnp.zeros((HALO, SSD_CONV_DIM), F32)
        st_ref[...] = jnp.zeros(st_ref.shape, F32)

    @pl.when(step != 0)
    def _():
        ext_ref[0:HALO, :] = ext_ref[rows:rows + HALO, :]

    ext_ref[HALO:HALO + rows, :] = xbc_ref[...]

    ext = ext_ref[...]
    conv = cb_ref[...] + cw_ref[SSD_CONV - 1:SSD_CONV, :] * ext[HALO:]
    for k in range(SSD_CONV - 1):
        shift = SSD_CONV - 1 - k
        conv = conv + cw_ref[k:k + 1, :] * pltpu.roll(ext, shift, axis=0)[HALO:]
    xact_all = _silu(conv)

    dt_all = _softplus(dt_ref[...] + dtb_ref[...])
    adt_all = dt_all * (-LOG2E * jnp.exp(alog_ref[...]))
    row = lax.broadcasted_iota(jnp.int32, (CHUNK, CHUNK), 0)
    col = lax.broadcasted_iota(jnp.int32, (CHUNK, CHUNK), 1)
    causal = row >= col
    tri = jnp.where(causal, 1.0, 0.0).astype(BF16)
    tri3 = jnp.concatenate([tri] * SPLIT_TERMS, axis=1)
    lane = lax.broadcasted_iota(jnp.int32, (CHUNK, SSD_WIDTH), 1)
    low_half = (lane % LANES) < SSD_HEAD_DIM
    copy = lax.broadcasted_iota(jnp.int32, (2 * CHUNK, LANES), 1) // SSD_HEADS
    heads_per_group = SSD_HEADS // SSD_GROUPS

    for ci in range(rows // CHUNK):
        rs = slice(ci * CHUNK, (ci + 1) * CHUNK)
        xact = xact_all[rs]
        xs = xact[:, :SSD_WIDTH]
        dt = dt_all[rs]
        a_cum = jnp.dot(tri3, jnp.concatenate(
            [t.astype(BF16) for t in _split_f32(adt_all[rs])], axis=0),
            preferred_element_type=F32)
        a_cum_t = a_cum.T

        terms = _split_f32(jnp.concatenate([dt, a_cum], axis=0))
        packed = jnp.where(copy == 0, terms[0], jnp.where(copy == 1, terms[1], terms[2]))
        expanded = jnp.dot(packed.astype(BF16), e_ref[...],
                           preferred_element_type=F32)
        dt_exp = expanded[:CHUNK]
        acum_exp = expanded[CHUNK:]
        alast_exp = acum_exp[CHUNK - 1:CHUNK, :]
        exp_a = jnp.exp2(acum_exp)
        decay_to_end = jnp.exp2(alast_exp - acum_exp)
        chunk_decay = jnp.exp2(alast_exp)

        xdt = xs * dt_exp
        xdt_lo = jnp.where(low_half, xdt, 0.0).astype(BF16)
        xdt_hi = jnp.where(low_half, 0.0, xdt).astype(BF16)
        xdte = (xdt * decay_to_end).astype(BF16)

        for g in range(SSD_GROUPS):
            gs = slice(g * SSD_GROUP_WIDTH, (g + 1) * SSD_GROUP_WIDTH)
            b_g = xact[:, SSD_WIDTH + g * SSD_STATE:SSD_WIDTH + (g + 1) * SSD_STATE]
            c_g = xact[:, SSD_WIDTH + SSD_BC_DIM + g * SSD_STATE:
                       SSD_WIDTH + SSD_BC_DIM + (g + 1) * SSD_STATE].astype(BF16)
            b_gt = b_g.T.astype(BF16)
            cb = jnp.dot(c_g, b_gt, preferred_element_type=F32)

            st_old = st_ref[g]
            y_off = jnp.dot(c_g, st_old.astype(BF16), preferred_element_type=F32)
            y_off = y_off * exp_a[:, gs]

            for jp in range(heads_per_group // 2):
                blk = g * (heads_per_group // 2) + jp
                bs = slice(blk * LANES, (blk + 1) * LANES)
                ms = []
                for hh in range(2):
                    h = 2 * blk + hh
                    seg = a_cum[:, h:h + 1] - a_cum_t[h:h + 1, :]
                    decay = jnp.exp2(jnp.where(causal, seg, -jnp.inf))
                    ms.append((cb * decay).astype(BF16))
                lhs = jnp.concatenate(ms, axis=1)
                rhs = jnp.concatenate([xdt_lo[:, bs], xdt_hi[:, bs]], axis=0)
                y_diag = jnp.dot(lhs, rhs, preferred_element_type=F32)
                y_ref[rs, bs] = (y_diag + y_off[:, jp * LANES:(jp + 1) * LANES]
                                 + dskip_ref[:, bs] * xs[:, bs])

            st_ref[g] = (st_old * chunk_decay[:, gs]
                         + jnp.dot(b_gt, xdte[:, gs], preferred_element_type=F32))

    for g in range(SSD_GROUPS):
        gs = slice(g * SSD_GROUP_WIDTH, (g + 1) * SSD_GROUP_WIDTH)
        yz = y_ref[:, gs] * zb_ref[:, gs].astype(F32)
        ms = jnp.mean(yz * yz, axis=-1, keepdims=True)
        o_ref[:, gs] = (yz * lax.rsqrt(ms + NORM_EPS) * ng_ref[:, gs]).astype(o_ref.dtype)


def _ssd(uvz, xbc, dt_raw, conv_w, conv_b, dt_bias_p, a_log_p, d_skip_exp, norm_g,
         expand, casts, *, batch, rows):
    m = uvz.shape[0]
    ns = m // batch // rows
    full = lambda shape: pl.BlockSpec(shape, lambda b, c: (0,) * len(shape))
    cast_specs = []
    for w in casts:
        assert w.shape[0] % (batch * ns * 16) == 0
        cast_specs.append(pl.BlockSpec((w.shape[0] // (batch * ns), w.shape[1]),
                                       lambda b, c: (b * ns + c, 0)))
    return pl.pallas_call(
        functools.partial(_ssd_kernel, rows=rows, n_casts=len(casts)),
        grid=(batch, ns),
        in_specs=[
            pl.BlockSpec((rows, SSD_WIDTH), lambda b, c: (b * ns + c, 3)),
            pl.BlockSpec((rows, SSD_CONV_DIM), lambda b, c: (b * ns + c, 0)),
            pl.BlockSpec((rows, LANES), lambda b, c: (b * ns + c, 0)),
            full((SSD_CONV, SSD_CONV_DIM)),
            full((1, SSD_CONV_DIM)),
            full((1, LANES)),
            full((1, LANES)),
            full((1, SSD_WIDTH)),
            full((1, SSD_WIDTH)),
            full((LANES, SSD_WIDTH)),
        ] + cast_specs,
        out_specs=[pl.BlockSpec((rows, SSD_WIDTH), lambda b, c: (b * ns + c, 0))]
        + cast_specs,
        out_shape=[jax.ShapeDtypeStruct((m, SSD_WIDTH), BF16)]
        + [jax.ShapeDtypeStruct(w.shape, BF16) for w in casts],
        scratch_shapes=[
            pltpu.VMEM((rows + HALO, SSD_CONV_DIM), F32),
            pltpu.VMEM((SSD_GROUPS, SSD_STATE, SSD_GROUP_WIDTH), F32),
            pltpu.VMEM((rows, SSD_WIDTH), F32),
        ],
        compiler_params=pltpu.CompilerParams(
            dimension_semantics=("arbitrary", "arbitrary"),
            vmem_limit_bytes=VMEM_LIMIT),
        name="ssd",
    )(uvz, xbc, dt_raw, conv_w, conv_b.reshape(1, -1), dt_bias_p, a_log_p, d_skip_exp,
      norm_g.reshape(1, -1), expand, *casts)


def _out_proj2_kernel(ya_ref, yb_ref, wa_ref, wb_ref, x_ref, o_ref):
    acc = jnp.dot(ya_ref[...], wa_ref[...], preferred_element_type=F32)
    acc = acc + jnp.dot(yb_ref[...], wb_ref[...], preferred_element_type=F32)
    o_ref[...] = x_ref[...] + acc


def _out_proj2(ya, yb, w, x, *, tm, tn):
    m, k = ya.shape
    n = w.shape[1]
    assert w.shape[0] == 2 * k
    return pl.pallas_call(
        _out_proj2_kernel,
        grid=(m // tm, n // tn),
        in_specs=[
            pl.BlockSpec((tm, k), lambda i, j: (i, 0)),
            pl.BlockSpec((tm, k), lambda i, j: (i, 0)),
            pl.BlockSpec((k, tn), lambda i, j: (0, j)),
            pl.BlockSpec((k, tn), lambda i, j: (1, j)),
            pl.BlockSpec((tm, tn), lambda i, j: (i, j)),
        ],
        out_specs=pl.BlockSpec((tm, tn), lambda i, j: (i, j)),
        out_shape=jax.ShapeDtypeStruct((m, n), F32),
        compiler_params=pltpu.CompilerParams(
            dimension_semantics=("arbitrary", "arbitrary"),
            vmem_limit_bytes=VMEM_LIMIT),
        name="out_proj_even",
    )(ya, yb, w, w, x)


def _out_proj_norm_kernel(y_ref, w_ref, x_ref, g_ref, o_ref):
    x2 = x_ref[...] + jnp.dot(y_ref[...], w_ref[...], preferred_element_type=F32)
    ms = jnp.mean(x2 * x2, axis=-1, keepdims=True)
    o_ref[...] = x2 * lax.rsqrt(ms + NORM_EPS) * g_ref[...]


def _out_proj_norm(y, w, x, g, *, tm):
    m, k = y.shape
    n = w.shape[1]
    return pl.pallas_call(
        _out_proj_norm_kernel,
        grid=(m // tm,),
        in_specs=[
            pl.BlockSpec((tm, k), lambda i: (i, 0)),
            pl.BlockSpec((k, n), lambda i: (0, 0)),
            pl.BlockSpec((tm, n), lambda i: (i, 0)),
            pl.BlockSpec((1, n), lambda i: (0, 0)),
        ],
        out_specs=pl.BlockSpec((tm, n), lambda i: (i, 0)),
        out_shape=jax.ShapeDtypeStruct((m, n), F32),
        compiler_params=pltpu.CompilerParams(
            dimension_semantics=("arbitrary",), vmem_limit_bytes=VMEM_LIMIT),
        name="out_proj_odd_norm",
    )(y, w, x, g.reshape(1, n))


def _diff_attn_kernel(slopes_ref, q_ref, k_ref, v_ref, gate_ref, lq1_ref, lk1_ref,
                      lq2_ref, lk2_ref, sg_ref, o_ref, ka_ref, kb_ref, va_ref, *, tq, hps):
    hp = pl.program_id(1)

    @pl.when((pl.program_id(0) == 0) & (hp == 0))
    def _():
        for hh in range(hps):
            va_ref[hh, :, DIFF_V_DIM:2 * DIFF_V_DIM] = jnp.ones((SEQ, DIFF_V_DIM), BF16)

    tq2 = 2 * tq
    lane = lax.broadcasted_iota(jnp.int32, (tq2, LANES), 1)
    first_map = lane < DIFF_HEAD_DIM
    sub = lane % DIFF_HEAD_DIM
    rows = lax.broadcasted_iota(jnp.int32, (tq2, LANES), 0)
    ones_mask = jnp.where(sub < SPLIT_TERMS, 1.0, 0.0)
    row = lax.broadcasted_iota(jnp.int32, (tq, tq), 0)
    col = lax.broadcasted_iota(jnp.int32, (tq, tq), 1)
    keep = row >= col
    nt_dims = (((1,), (1,)), ((), ()))
    lam = (jnp.exp(jnp.sum(lq1_ref[...] * lk1_ref[...], axis=-1, keepdims=True))
           - jnp.exp(jnp.sum(lq2_ref[...] * lk2_ref[...], axis=-1, keepdims=True))
           + LAMBDA_INIT)

    def probs(s, nk):
        s_diag = jnp.where(keep, s[:, nk - tq:], -jnp.inf)
        m = jnp.max(s_diag, axis=-1, keepdims=True)
        if nk == tq:
            return jnp.exp2(s_diag - m).astype(BF16)
        s_past = s[:, :nk - tq]
        m = jnp.maximum(m, jnp.max(s_past, axis=-1, keepdims=True))
        return jnp.concatenate([jnp.exp2(s_past - m), jnp.exp2(s_diag - m)],
                               axis=1).astype(BF16)

    for qp in range(SEQ // tq2):
        for hh in range(hps):
            hl = slice(hh * LANES, (hh + 1) * LANES)
            slope2 = slopes_ref[hp * hps + hh] * LOG2E
            qs2 = slice(qp * tq2, (qp + 1) * tq2)
            nk2 = (qp + 1) * tq2
            t1, t2, t3 = _split_f32(slope2 * (rows + qs2.start).astype(F32))
            tab = jnp.where(sub == 0, t1, jnp.where(sub == 1, t2, jnp.where(sub == 2, t3, 0.0)))
            k = k_ref[qs2, hl].astype(F32)
            ka_ref[hh, qs2, :] = jnp.where(first_map, k, tab).astype(BF16)
            kb_ref[hh, qs2, :] = jnp.where(first_map, tab, k).astype(BF16)
            va_ref[hh, qs2, 0:DIFF_V_DIM] = v_ref[qs2, hl]

            qf = q_ref[qs2, hl].astype(F32)
            qa = jnp.where(first_map, qf, ones_mask).astype(BF16)
            qb = jnp.where(first_map, ones_mask, qf).astype(BF16)
            s_a = lax.dot_general(qa, ka_ref[hh, 0:nk2, :], nt_dims, preferred_element_type=F32)
            s_b = lax.dot_general(qb, kb_ref[hh, 0:nk2, :], nt_dims, preferred_element_type=F32)
            for t in range(2):
                qs = slice(qp * tq2 + t * tq, qp * tq2 + (t + 1) * tq)
                ts = slice(t * tq, (t + 1) * tq)
                nk = qs.stop
                p = jnp.concatenate([probs(s_a[ts, :nk], nk), probs(s_b[ts, :nk], nk)], axis=0)
                acc = jnp.dot(p, va_ref[hh, 0:nk, :], preferred_element_type=F32)
                att = acc[:, :DIFF_V_DIM] / acc[:, DIFF_V_DIM:DIFF_V_DIM + 1]
                o = att[:tq] - lam * att[tq:]
                ms = jnp.mean(o * o, axis=-1, keepdims=True)
                o = o * lax.rsqrt(ms + NORM_EPS) * sg_ref[...]
                o = o * (1.0 - LAMBDA_INIT)
                o_ref[qs, hl] = (o * gate_ref[qs, hl].astype(F32)).astype(o_ref.dtype)


def _diff_attn(qkvg, slopes, lq1, lk1, lq2, lk2, subln_g, *, batch, tq, hps):
    m = qkvg.shape[0]
    w = hps * LANES
    nb = DIFF_HEADS // hps
    vec = lambda n: pl.BlockSpec((1, n), lambda b, h, s: (0, 0))
    grid_spec = pltpu.PrefetchScalarGridSpec(
        num_scalar_prefetch=1,
        grid=(batch, nb),
        in_specs=[
            pl.BlockSpec((SEQ, w), lambda b, h, s: (b, h)),
            pl.BlockSpec((SEQ, w), lambda b, h, s: (b, nb + h)),
            pl.BlockSpec((SEQ, w), lambda b, h, s: (b, 2 * nb + h)),
            pl.BlockSpec((SEQ, w), lambda b, h, s: (b, 3 * nb + h)),
            vec(DIFF_HEAD_DIM), vec(DIFF_HEAD_DIM), vec(DIFF_HEAD_DIM),
            vec(DIFF_HEAD_DIM), vec(DIFF_V_DIM),
        ],
        out_specs=pl.BlockSpec((SEQ, w), lambda b, h, s: (b, h)),
        scratch_shapes=[
            pltpu.VMEM((hps, SEQ, LANES), BF16),
            pltpu.VMEM((hps, SEQ, LANES), BF16),
            pltpu.VMEM((hps, SEQ, 2 * DIFF_V_DIM), BF16),
        ],
    )
    return pl.pallas_call(
        functools.partial(_diff_attn_kernel, tq=tq, hps=hps),
        grid_spec=grid_spec,
        out_shape=jax.ShapeDtypeStruct((m, DIFF_WIDTH), BF16),
        compiler_params=pltpu.CompilerParams(
            dimension_semantics=("arbitrary", "arbitrary"),
            vmem_limit_bytes=VMEM_LIMIT),
        name="diff_attn",
    )(slopes, qkvg, qkvg, qkvg, qkvg, lq1.reshape(1, -1), lk1.reshape(1, -1),
      lq2.reshape(1, -1), lk2.reshape(1, -1), subln_g.reshape(1, -1))


def kernel(x, l0_norm_g, l0_w_in, l0_gmlp_ln_g, l0_gmlp_ln_b, l0_spatial_w,
           l0_spatial_b, l0_conv_w, l0_conv_b, l0_dt_bias, l0_a_log, l0_d_skip,
           l0_ssm_norm_g, l0_w_out, l1_norm_g, l1_w_in, l1_lambda_q1,
           l1_lambda_k1, l1_lambda_q2, l1_lambda_k2, l1_subln_g, l1_w_out,
           final_norm_g):
    batch, seq, d = x.shape
    m = batch * seq
    x2d = x.reshape(m, d)

    def head_lanes(v):
        v = jnp.concatenate([v] * SPLIT_TERMS, axis=0)
        return jnp.pad(v, [(0, LANES - SPLIT_TERMS * SSD_HEADS)] + [(0, 0)] * (v.ndim - 1))

    w_in0_t = l0_w_in.T.astype(BF16)
    w_dt_t = head_lanes(w_in0_t[EVEN_MAIN:])
    uvz, xbc, dt_raw = _norm_matmul(
        x2d, l0_norm_g, w_in0_t,
        [(BF16, [(2 * GMLP_WIDTH, "gelu"), (GMLP_WIDTH + SSD_WIDTH, "silu")]),
         (F32, [(SSD_CONV_DIM, None)])],
        tm=PROJ_TM, tn=EVEN_PROJ_TN, w_small=w_dt_t, w_transposed=True)

    spatial_b_exp = jnp.repeat(l0_spatial_b.T, LANES, axis=1)
    y_a = _gmlp(uvz, l0_gmlp_ln_g, l0_gmlp_ln_b, l0_spatial_w, spatial_b_exp,
                rows=GMLP_ROWS)

    dt_bias_p = head_lanes(l0_dt_bias).reshape(1, LANES)
    a_log_p = head_lanes(l0_a_log).reshape(1, LANES)
    d_skip_exp = jnp.repeat(l0_d_skip, SSD_HEAD_DIM).reshape(1, SSD_WIDTH)
    lane_id = jnp.arange(LANES)
    expand = ((lane_id[:, None] % SSD_HEADS
               == (jnp.arange(SSD_WIDTH) // SSD_HEAD_DIM)[None, :])
              & (lane_id[:, None] < SPLIT_TERMS * SSD_HEADS)).astype(BF16)
    y_b, w_out0, w_in1, w_out1 = _ssd(
        uvz, xbc, dt_raw, l0_conv_w, l0_conv_b, dt_bias_p, a_log_p, d_skip_exp,
        l0_ssm_norm_g, expand, [l0_w_out, l1_w_in, l1_w_out], batch=batch, rows=SSD_ROWS)

    x1 = _out_proj2(y_a, y_b, w_out0, x2d, tm=PROJ_TM, tn=OUT_EVEN_TN)

    q_scale = LOG2E * DIFF_HEAD_DIM ** -0.5
    (qkvg,) = _norm_matmul(
        x1, l1_norm_g, w_in1,
        [(BF16, [(DIFF_WIDTH, ("scale", q_scale)), (2 * DIFF_WIDTH, None),
                 (DIFF_WIDTH, "silu")])], tm=PROJ_TM, tn=ODD_PROJ_TN)
    slopes = 2.0 ** (-8.0 * (jnp.arange(DIFF_HEADS, dtype=F32) + 1.0) / DIFF_HEADS)
    o = _diff_attn(qkvg, slopes, l1_lambda_q1, l1_lambda_k1, l1_lambda_q2,
                   l1_lambda_k2, l1_subln_g, batch=batch, tq=ATTN_TQ,
                   hps=ATTN_HEADS_PER_STEP)

    out = _out_proj_norm(o, w_out1, x1, final_norm_g, tm=OUT_ODD_TM)
    return out.reshape(batch, seq, d)
```

```python
import functools
import math

import jax
import jax.numpy as jnp
from jax import lax
from jax.experimental import pallas as pl
from jax.experimental.pallas import tpu as pltpu

F32 = jnp.float32
BF16 = jnp.bfloat16

D_MODEL = 2048
SEQ = 2048
CHUNK = 128
NORM_EPS = 1e-5

GMLP_WIDTH = 2048
GMLP_GROUPS = 16
SSD_WIDTH = 2048
SSD_HEADS = 32
SSD_HEAD_DIM = 64
SSD_GROUPS = 4
SSD_STATE = 128
SSD_CONV = 4
SSD_BC_DIM = SSD_GROUPS * SSD_STATE
SSD_CONV_DIM = SSD_WIDTH + 2 * SSD_BC_DIM
SSD_GROUP_WIDTH = SSD_WIDTH // SSD_GROUPS
UVZ_WIDTH = 3 * GMLP_WIDTH + SSD_WIDTH
EVEN_MAIN = UVZ_WIDTH + SSD_CONV_DIM

DIFF_HEADS = 16
DIFF_HEAD_DIM = 64
DIFF_V_DIM = 128
DIFF_WIDTH = DIFF_HEADS * DIFF_V_DIM
LAMBDA_INIT = 0.8 - 0.6 * math.exp(-0.3 * 1)
LOG2E = math.log2(math.e)

LANES = 128
HALO = 8
VMEM_LIMIT = 56 * 1024 * 1024

PROJ_TM = 1024
EVEN_PROJ_TN = 1024
ODD_PROJ_TN = 2048
OUT_EVEN_TN = 1024
OUT_ODD_TM = 512
GMLP_ROWS = 1024
SSD_ROWS = 512
ATTN_TQ = 256
ATTN_HEADS_PER_STEP = 4


def _silu(x):
    half = 0.5 * x
    return half + half * jnp.tanh(half)


SPLIT_TERMS = 3


def _split_f32(x):
    t1 = x.astype(BF16).astype(F32)
    r1 = x - t1
    t2 = r1.astype(BF16).astype(F32)
    t3 = (r1 - t2).astype(BF16).astype(F32)
    return t1, t2, t3


def _gelu_tanh(x):
    c = math.sqrt(2.0 / math.pi)
    half = 0.5 * x
    return half + half * jnp.tanh(x * (c + (c * 0.044715) * (x * x)))


def _softplus(x):
    return jnp.maximum(x, 0.0) + jnp.log1p(jnp.exp(-jnp.abs(x)))


def _norm_matmul_kernel(*refs, bounds, acts, has_small, w_transposed):
    dims = (((1,), (1,)), ((), ())) if w_transposed else (((1,), (0,)), ((), ()))
    refs = list(refs)
    x_ref, g_ref, w_ref = refs[:3]
    del refs[:3]
    ws_ref = refs.pop(0) if has_small else None
    out_refs = [refs.pop(0) for _ in bounds]
    small_ref = refs.pop(0) if has_small else None
    h_ref = refs.pop(0)

    j = pl.program_id(1)

    @pl.when(j == 0)
    def _():
        xf = x_ref[...]
        ms = jnp.mean(xf * xf, axis=-1, keepdims=True)
        h = (xf * lax.rsqrt(ms + NORM_EPS) * g_ref[...]).astype(BF16)
        h_ref[...] = h
        if has_small:
            small_ref[...] = lax.dot_general(h, ws_ref[...], dims,
                                             preferred_element_type=F32)

    branches = []
    for (lo, _), seg_acts, o_ref in zip(bounds, acts, out_refs):
        for nblk, act in seg_acts:
            branches.append((lo, lo + nblk, act, o_ref))
            lo += nblk
    for lo, hi, act, o_ref in branches:
        @pl.when((j >= lo) & (j < hi))
        def _(act=act, o_ref=o_ref):
            acc = lax.dot_general(h_ref[...], w_ref[...], dims, preferred_element_type=F32)
            if act is None:
                val = acc
            elif act == "gelu":
                val = _gelu_tanh(acc)
            elif act == "silu":
                val = _silu(acc)
            else:
                kind, c = act
                assert kind == "scale"
                val = acc * c
            o_ref[...] = val.astype(o_ref.dtype)


def _norm_matmul(x, g, w, segs, *, tm, tn, w_small=None, w_transposed=False):
    m, k = x.shape
    assert m % tm == 0
    bounds = []
    acts = []
    lo = 0
    for _, ranges in segs:
        start = lo
        for ncols, act in ranges:
            assert ncols % tn == 0
            lo += ncols
        bounds.append((start // tn, lo // tn))
        acts.append(tuple((ncols // tn, act) for ncols, act in ranges))
    n = lo
    n_axis = 0 if w_transposed else 1
    assert n <= w.shape[n_axis] and w.shape[1 - n_axis] == k

    if w_transposed:
        w_spec = pl.BlockSpec((tn, k), lambda i, j: (j, 0))
    else:
        w_spec = pl.BlockSpec((k, tn), lambda i, j: (0, j))
    in_specs = [
        pl.BlockSpec((tm, k), lambda i, j: (i, 0)),
        pl.BlockSpec((1, k), lambda i, j: (0, 0)),
        w_spec,
    ]
    args = [x, g.reshape(1, k), w]
    if w_small is not None:
        in_specs.append(pl.BlockSpec(w_small.shape, lambda i, j: (0, 0)))
        args.append(w_small)

    out_specs = []
    out_shape = []
    for (blo, bhi), (dtype, _) in zip(bounds, segs):
        def idx(i, j, blo=blo, bhi=bhi):
            return (i, jnp.clip(j - blo, 0, bhi - blo - 1))
        out_specs.append(pl.BlockSpec((tm, tn), idx))
        out_shape.append(jax.ShapeDtypeStruct((m, (bhi - blo) * tn), dtype))
    if w_small is not None:
        ns = w_small.shape[n_axis]
        out_specs.append(pl.BlockSpec((tm, ns), lambda i, j: (i, 0)))
        out_shape.append(jax.ShapeDtypeStruct((m, ns), F32))

    return pl.pallas_call(
        functools.partial(_norm_matmul_kernel, bounds=tuple(bounds),
                          acts=tuple(acts),
                          has_small=w_small is not None, w_transposed=w_transposed),
        grid=(m // tm, n // tn),
        in_specs=in_specs,
        out_specs=out_specs,
        out_shape=out_shape,
        scratch_shapes=[pltpu.VMEM((tm, k), BF16)],
        compiler_params=pltpu.CompilerParams(
            dimension_semantics=("arbitrary", "arbitrary"),
            vmem_limit_bytes=VMEM_LIMIT),
        name="norm_matmul",
    )(*args)


def _gmlp_kernel(u_ref, v_ref, z_ref, lng_ref, lnb_ref, ws_ref, sb_ref, o_ref,
                 vn_ref, *, rows):
    v = v_ref[...].astype(F32)
    mu = jnp.mean(v, axis=-1, keepdims=True)
    vc = v - mu
    var = jnp.mean(vc * vc, axis=-1, keepdims=True)
    vn = vc * lax.rsqrt(var + NORM_EPS) * lng_ref[...] + lnb_ref[...]
    vn_ref[...] = vn.astype(BF16)

    row = lax.broadcasted_iota(jnp.int32, (CHUNK, CHUNK), 0)
    col = lax.broadcasted_iota(jnp.int32, (CHUNK, CHUNK), 1)
    causal = row >= col
    for g in range(GMLP_GROUPS):
        cs = slice(g * LANES, (g + 1) * LANES)
        w_s = jnp.where(causal, ws_ref[g], 0.0).astype(BF16)
        for c in range(rows // CHUNK):
            rs = slice(c * CHUNK, (c + 1) * CHUNK)
            v_mix = jnp.dot(w_s, vn_ref[rs, cs], preferred_element_type=F32)
            v_mix = v_mix + sb_ref[:, cs]
            o_ref[rs, cs] = (u_ref[rs, cs].astype(F32) * v_mix
                             * z_ref[rs, cs].astype(F32)).astype(o_ref.dtype)


def _gmlp(uvz, ln_g, ln_b, spatial_w, spatial_b_exp, *, rows):
    m = uvz.shape[0]
    w = GMLP_WIDTH
    return pl.pallas_call(
        functools.partial(_gmlp_kernel, rows=rows),
        grid=(m // rows,),
        in_specs=[
            pl.BlockSpec((rows, w), lambda i: (i, 0)),
            pl.BlockSpec((rows, w), lambda i: (i, 1)),
            pl.BlockSpec((rows, w), lambda i: (i, 2)),
            pl.BlockSpec((1, w), lambda i: (0, 0)),
            pl.BlockSpec((1, w), lambda i: (0, 0)),
            pl.BlockSpec((GMLP_GROUPS, CHUNK, CHUNK), lambda i: (0, 0, 0)),
            pl.BlockSpec((CHUNK, w), lambda i: (0, 0)),
        ],
        out_specs=pl.BlockSpec((rows, w), lambda i: (i, 0)),
        out_shape=jax.ShapeDtypeStruct((m, w), BF16),
        scratch_shapes=[pltpu.VMEM((rows, w), BF16)],
        compiler_params=pltpu.CompilerParams(
            dimension_semantics=("arbitrary",), vmem_limit_bytes=VMEM_LIMIT),
        name="gmlp",
    )(uvz, uvz, uvz, ln_g.reshape(1, w), ln_b.reshape(1, w), spatial_w,
      spatial_b_exp)


def _ssd_kernel(zb_ref, xbc_ref, dt_ref, cw_ref, cb_ref, dtb_ref, alog_ref,
                dskip_ref, ng_ref, e_ref, *rest, rows, n_casts):
    cast_in = rest[:n_casts]
    o_ref = rest[n_casts]
    cast_out = rest[n_casts + 1:2 * n_casts + 1]
    ext_ref, st_ref, y_ref = rest[2 * n_casts + 1:]
    step = pl.program_id(1)

    for wi_ref, wo_ref in zip(cast_in, cast_out):
        wo_ref[...] = wi_ref[...].astype(wo_ref.dtype)

    @pl.when(step == 0)
    def _():
        ext_ref[0:HALO, :] = jnp.zeros((HALO, SSD_CONV_DIM), F32)
        st_ref[...] = jnp.zeros(st_ref.shape, F32)

    @pl.when(step != 0)
    def _():
        ext_ref[0:HALO, :] = ext_ref[rows:rows + HALO, :]

    ext_ref[HALO:HALO + rows, :] = xbc_ref[...]

    ext = ext_ref[...]
    conv = cb_ref[...] + cw_ref[SSD_CONV - 1:SSD_CONV, :] * ext[HALO:]
    for k in range(SSD_CONV - 1):
        shift = SSD_CONV - 1 - k
        conv = conv + cw_ref[k:k + 1, :] * pltpu.roll(ext, shift, axis=0)[HALO:]
    xact_all = _silu(conv)

    dt_all = _softplus(dt_ref[...] + dtb_ref[...])
    adt_all = dt_all * (-LOG2E * jnp.exp(alog_ref[...]))
    row = lax.broadcasted_iota(jnp.int32, (CHUNK, CHUNK), 0)
    col = lax.broadcasted_iota(jnp.int32, (CHUNK, CHUNK), 1)
    causal = row >= col
    tri = jnp.where(causal, 1.0, 0.0).astype(BF16)
    tri3 = jnp.concatenate([tri] * SPLIT_TERMS, axis=1)
    lane = lax.broadcasted_iota(jnp.int32, (CHUNK, SSD_WIDTH), 1)
    low_half = (lane % LANES) < SSD_HEAD_DIM
    copy = lax.broadcasted_iota(jnp.int32, (2 * CHUNK, LANES), 1) // SSD_HEADS
    heads_per_group = SSD_HEADS // SSD_GROUPS

    for ci in range(rows // CHUNK):
        rs = slice(ci * CHUNK, (ci + 1) * CHUNK)
        xact = xact_all[rs]
        xs = xact[:, :SSD_WIDTH]
        dt = dt_all[rs]
        a_cum = jnp.dot(tri3, jnp.concatenate(
            [t.astype(BF16) for t in _split_f32(adt_all[rs])], axis=0),
            preferred_element_type=F32)
        a_cum_t = a_cum.T

        terms = _split_f32(jnp.concatenate([dt, a_cum], axis=0))
        packed = jnp.where(copy == 0, terms[0], jnp.where(copy == 1, terms[1], terms[2]))
        expanded = jnp.dot(packed.astype(BF16), e_ref[...],
                           preferred_element_type=F32)
        dt_exp = expanded[:CHUNK]
        acum_exp = expanded[CHUNK:]
        alast_exp = acum_exp[CHUNK - 1:CHUNK, :]
        exp_a = jnp.exp2(acum_exp)
        decay_to_end = jnp.exp2(alast_exp - acum_exp)
        chunk_decay = jnp.exp2(alast_exp)

        xdt = xs * dt_exp
        xdt_lo = jnp.where(low_half, xdt, 0.0).astype(BF16)
        xdt_hi = jnp.where(low_half, 0.0, xdt).astype(BF16)
        xdte = (xdt * decay_to_end).astype(BF16)

        for g in range(SSD_GROUPS):
            gs = slice(g * SSD_GROUP_WIDTH, (g + 1) * SSD_GROUP_WIDTH)
            b_g = xact[:, SSD_WIDTH + g * SSD_STATE:SSD_WIDTH + (g + 1) * SSD_STATE]
            c_g = xact[:, SSD_WIDTH + SSD_BC_DIM + g * SSD_STATE:
                       SSD_WIDTH + SSD_BC_DIM + (g + 1) * SSD_STATE].astype(BF16)
            b_gt = b_g.T.astype(BF16)
            cb = jnp.dot(c_g, b_gt, preferred_element_type=F32)

            st_old = st_ref[g]
            y_off = jnp.dot(c_g, st_old.astype(BF16), preferred_element_type=F32)
            y_off = y_off * exp_a[:, gs]

            for jp in range(heads_per_group // 2):
                blk = g * (heads_per_group // 2) + jp
                bs = slice(blk * LANES, (blk + 1) * LANES)
                ms = []
                for hh in range(2):
                    h = 2 * blk + hh
                    seg = a_cum[:, h:h + 1] - a_cum_t[h:h + 1, :]
                    decay = jnp.exp2(jnp.where(causal, seg, -jnp.inf))
                    ms.append((cb * decay).astype(BF16))
                lhs = jnp.concatenate(ms, axis=1)
                rhs = jnp.concatenate([xdt_lo[:, bs], xdt_hi[:, bs]], axis=0)
                y_diag = jnp.dot(lhs, rhs, preferred_element_type=F32)
                y_ref[rs, bs] = (y_diag + y_off[:, jp * LANES:(jp + 1) * LANES]
                                 + dskip_ref[:, bs] * xs[:, bs])

            st_ref[g] = (st_old * chunk_decay[:, gs]
                         + jnp.dot(b_gt, xdte[:, gs], preferred_element_type=F32))

    for g in range(SSD_GROUPS):
        gs = slice(g * SSD_GROUP_WIDTH, (g + 1) * SSD_GROUP_WIDTH)
        yz = y_ref[:, gs] * zb_ref[:, gs].astype(F32)
        ms = jnp.mean(yz * yz, axis=-1, keepdims=True)
        o_ref[:, gs] = (yz * lax.rsqrt(ms + NORM_EPS) * ng_ref[:, gs]).astype(o_ref.dtype)


def _ssd(uvz, xbc, dt_raw, conv_w, conv_b, dt_bias_p, a_log_p, d_skip_exp, norm_g,
         expand, casts, *, batch, rows):
    m = uvz.shape[0]
    ns = m // batch // rows
    full = lambda shape: pl.BlockSpec(shape, lambda b, c: (0,) * len(shape))
    cast_specs = []
    for w in casts:
        assert w.shape[0] % (batch * ns * 16) == 0
        cast_specs.append(pl.BlockSpec((w.shape[0] // (batch * ns), w.shape[1]),
                                       lambda b, c: (b * ns + c, 0)))
    return pl.pallas_call(
        functools.partial(_ssd_kernel, rows=rows, n_casts=len(casts)),
        grid=(batch, ns),
        in_specs=[
            pl.BlockSpec((rows, SSD_WIDTH), lambda b, c: (b * ns + c, 3)),
            pl.BlockSpec((rows, SSD_CONV_DIM), lambda b, c: (b * ns + c, 0)),
            pl.BlockSpec((rows, LANES), lambda b, c: (b * ns + c, 0)),
            full((SSD_CONV, SSD_CONV_DIM)),
            full((1, SSD_CONV_DIM)),
            full((1, LANES)),
            full((1, LANES)),
            full((1, SSD_WIDTH)),
            full((1, SSD_WIDTH)),
            full((LANES, SSD_WIDTH)),
        ] + cast_specs,
        out_specs=[pl.BlockSpec((rows, SSD_WIDTH), lambda b, c: (b * ns + c, 0))]
        + cast_specs,
        out_shape=[jax.ShapeDtypeStruct((m, SSD_WIDTH), BF16)]
        + [jax.ShapeDtypeStruct(w.shape, BF16) for w in casts],
        scratch_shapes=[
            pltpu.VMEM((rows + HALO, SSD_CONV_DIM), F32),
            pltpu.VMEM((SSD_GROUPS, SSD_STATE, SSD_GROUP_WIDTH), F32),
            pltpu.VMEM((rows, SSD_WIDTH), F32),
        ],
        compiler_params=pltpu.CompilerParams(
            dimension_semantics=("arbitrary", "arbitrary"),
            vmem_limit_bytes=VMEM_LIMIT),
        name="ssd",
    )(uvz, xbc, dt_raw, conv_w, conv_b.reshape(1, -1), dt_bias_p, a_log_p, d_skip_exp,
      norm_g.reshape(1, -1), expand, *casts)


def _out_proj2_kernel(ya_ref, yb_ref, wa_ref, wb_ref, x_ref, o_ref):
    acc = jnp.dot(ya_ref[...], wa_ref[...], preferred_element_type=F32)
    acc = acc + jnp.dot(yb_ref[...], wb_ref[...], preferred_element_type=F32)
    o_ref[...] = x_ref[...] + acc


def _out_proj2(ya, yb, w, x, *, tm, tn):
    m, k = ya.shape
    n = w.shape[1]
    assert w.shape[0] == 2 * k
    return pl.pallas_call(
        _out_proj2_kernel,
        grid=(m // tm, n // tn),
        in_specs=[
            pl.BlockSpec((tm, k), lambda i, j: (i, 0)),
            pl.BlockSpec((tm, k), lambda i, j: (i, 0)),
            pl.BlockSpec((k, tn), lambda i, j: (0, j)),
            pl.BlockSpec((k, tn), lambda i, j: (1, j)),
            pl.BlockSpec((tm, tn), lambda i, j: (i, j)),
        ],
        out_specs=pl.BlockSpec((tm, tn), lambda i, j: (i, j)),
        out_shape=jax.ShapeDtypeStruct((m, n), F32),
        compiler_params=pltpu.CompilerParams(
            dimension_semantics=("arbitrary", "arbitrary"),
            vmem_limit_bytes=VMEM_LIMIT),
        name="out_proj_even",
    )(ya, yb, w, w, x)


def _out_proj_norm_kernel(y_ref, w_ref, x_ref, g_ref, o_ref):
    x2 = x_ref[...] + jnp.dot(y_ref[...], w_ref[...], preferred_element_type=F32)
    ms = jnp.mean(x2 * x2, axis=-1, keepdims=True)
    o_ref[...] = x2 * lax.rsqrt(ms + NORM_EPS) * g_ref[...]


def _out_proj_norm(y, w, x, g, *, tm):
    m, k = y.shape
    n = w.shape[1]
    return pl.pallas_call(
        _out_proj_norm_kernel,
        grid=(m // tm,),
        in_specs=[
            pl.BlockSpec((tm, k), lambda i: (i, 0)),
            pl.BlockSpec((k, n), lambda i: (0, 0)),
            pl.BlockSpec((tm, n), lambda i: (i, 0)),
            pl.BlockSpec((1, n), lambda i: (0, 0)),
        ],
        out_specs=pl.BlockSpec((tm, n), lambda i: (i, 0)),
        out_shape=jax.ShapeDtypeStruct((m, n), F32),
        compiler_params=pltpu.CompilerParams(
            dimension_semantics=("arbitrary",), vmem_limit_bytes=VMEM_LIMIT),
        name="out_proj_odd_norm",
    )(y, w, x, g.reshape(1, n))


def _diff_attn_kernel(slopes_ref, q_ref, k_ref, v_ref, gate_ref, lq1_ref, lk1_ref,
                      lq2_ref, lk2_ref, sg_ref, o_ref, ka_ref, kb_ref, va_ref, *, tq, hps):
    hp = pl.program_id(1)

    @pl.when((pl.program_id(0) == 0) & (hp == 0))
    def _():
        for hh in range(hps):
            va_ref[hh, :, DIFF_V_DIM:2 * DIFF_V_DIM] = jnp.ones((SEQ, DIFF_V_DIM), BF16)

    lane = lax.broadcasted_iota(jnp.int32, (tq, LANES), 1)
    first_map = lane < DIFF_HEAD_DIM
    sub = lane % DIFF_HEAD_DIM
    rows = lax.broadcasted_iota(jnp.int32, (tq, LANES), 0)
    ones_mask = jnp.where(sub < SPLIT_TERMS, 1.0, 0.0)
    row = lax.broadcasted_iota(jnp.int32, (tq, tq), 0)
    col = lax.broadcasted_iota(jnp.int32, (tq, tq), 1)
    keep = row >= col
    nt_dims = (((1,), (1,)), ((), ()))
    lam = (jnp.exp(jnp.sum(lq1_ref[...] * lk1_ref[...], axis=-1, keepdims=True))
           - jnp.exp(jnp.sum(lq2_ref[...] * lk2_ref[...], axis=-1, keepdims=True))
           + LAMBDA_INIT)

    def probs(qq, kk, nk):
        s = lax.dot_general(qq, kk, nt_dims, preferred_element_type=F32)
        s_diag = jnp.where(keep, s[:, nk - tq:], -jnp.inf)
        m = jnp.max(s_diag, axis=-1, keepdims=True)
        if nk == tq:
            return jnp.exp2(s_diag - m).astype(BF16)
        s_past = s[:, :nk - tq]
        m = jnp.maximum(m, jnp.max(s_past, axis=-1, keepdims=True))
        return jnp.concatenate([jnp.exp2(s_past - m), jnp.exp2(s_diag - m)],
                               axis=1).astype(BF16)

    for qi in range(SEQ // tq):
        for hh in range(hps):
            hl = slice(hh * LANES, (hh + 1) * LANES)
            slope2 = slopes_ref[hp * hps + hh] * LOG2E
            qs = slice(qi * tq, (qi + 1) * tq)
            nk = (qi + 1) * tq
            t1, t2, t3 = _split_f32(slope2 * (rows + qs.start).astype(F32))
            tab = jnp.where(sub == 0, t1, jnp.where(sub == 1, t2, jnp.where(sub == 2, t3, 0.0)))
            k = k_ref[qs, hl].astype(F32)
            ka_ref[hh, qs, :] = jnp.where(first_map, k, tab).astype(BF16)
            kb_ref[hh, qs, :] = jnp.where(first_map, tab, k).astype(BF16)
            va_ref[hh, qs, 0:DIFF_V_DIM] = v_ref[qs, hl]

            qf = q_ref[qs, hl].astype(F32)
            qa = jnp.where(first_map, qf, ones_mask).astype(BF16)
            qb = jnp.where(first_map, ones_mask, qf).astype(BF16)
            p = jnp.concatenate([probs(qa, ka_ref[hh, 0:nk, :], nk),
                                 probs(qb, kb_ref[hh, 0:nk, :], nk)], axis=0)
            acc = jnp.dot(p, va_ref[hh, 0:nk, :], preferred_element_type=F32)
            att = acc[:, :DIFF_V_DIM] / acc[:, DIFF_V_DIM:DIFF_V_DIM + 1]
            o = att[:tq] - lam * att[tq:]
            ms = jnp.mean(o * o, axis=-1, keepdims=True)
            o = o * lax.rsqrt(ms + NORM_EPS) * sg_ref[...]
            o = o * (1.0 - LAMBDA_INIT)
            o_ref[qs, hl] = (o * gate_ref[qs, hl].astype(F32)).astype(o_ref.dtype)


def _diff_attn(qkvg, slopes, lq1, lk1, lq2, lk2, subln_g, *, batch, tq, hps):
    m = qkvg.shape[0]
    w = hps * LANES
    nb = DIFF_HEADS // hps
    vec = lambda n: pl.BlockSpec((1, n), lambda b, h, s: (0, 0))
    grid_spec = pltpu.PrefetchScalarGridSpec(
        num_scalar_prefetch=1,
        grid=(batch, nb),
        in_specs=[
            pl.BlockSpec((SEQ, w), lambda b, h, s: (b, h)),
            pl.BlockSpec((SEQ, w), lambda b, h, s: (b, nb + h)),
            pl.BlockSpec((SEQ, w), lambda b, h, s: (b, 2 * nb + h)),
            pl.BlockSpec((SEQ, w), lambda b, h, s: (b, 3 * nb + h)),
            vec(DIFF_HEAD_DIM), vec(DIFF_HEAD_DIM), vec(DIFF_HEAD_DIM),
            vec(DIFF_HEAD_DIM), vec(DIFF_V_DIM),
        ],
        out_specs=pl.BlockSpec((SEQ, w), lambda b, h, s: (b, h)),
        scratch_shapes=[
            pltpu.VMEM((hps, SEQ, LANES), BF16),
            pltpu.VMEM((hps, SEQ, LANES), BF16),
            pltpu.VMEM((hps, SEQ, 2 * DIFF_V_DIM), BF16),
        ],
    )
    return pl.pallas_call(
        functools.partial(_diff_attn_kernel, tq=tq, hps=hps),
        grid_spec=grid_spec,
        out_shape=jax.ShapeDtypeStruct((m, DIFF_WIDTH), BF16),
        compiler_params=pltpu.CompilerParams(
            dimension_semantics=("arbitrary", "arbitrary"),
            vmem_limit_bytes=VMEM_LIMIT),
        name="diff_attn",
    )(slopes, qkvg, qkvg, qkvg, qkvg, lq1.reshape(1, -1), lk1.reshape(1, -1),
      lq2.reshape(1, -1), lk2.reshape(1, -1), subln_g.reshape(1, -1))


def kernel(x, l0_norm_g, l0_w_in, l0_gmlp_ln_g, l0_gmlp_ln_b, l0_spatial_w,
           l0_spatial_b, l0_conv_w, l0_conv_b, l0_dt_bias, l0_a_log, l0_d_skip,
           l0_ssm_norm_g, l0_w_out, l1_norm_g, l1_w_in, l1_lambda_q1,
           l1_lambda_k1, l1_lambda_q2, l1_lambda_k2, l1_subln_g, l1_w_out,
           final_norm_g):
    batch, seq, d = x.shape
    m = batch * seq
    x2d = x.reshape(m, d)

    def head_lanes(v):
        v = jnp.concatenate([v] * SPLIT_TERMS, axis=0)
        return jnp.pad(v, [(0, LANES - SPLIT_TERMS * SSD_HEADS)] + [(0, 0)] * (v.ndim - 1))

    w_in0_t = l0_w_in.T.astype(BF16)
    w_dt_t = head_lanes(w_in0_t[EVEN_MAIN:])
    uvz, xbc, dt_raw = _norm_matmul(
        x2d, l0_norm_g, w_in0_t,
        [(BF16, [(2 * GMLP_WIDTH, "gelu"), (GMLP_WIDTH + SSD_WIDTH, "silu")]),
         (F32, [(SSD_CONV_DIM, None)])],
        tm=PROJ_TM, tn=EVEN_PROJ_TN, w_small=w_dt_t, w_transposed=True)

    spatial_b_exp = jnp.repeat(l0_spatial_b.T, LANES, axis=1)
    y_a = _gmlp(uvz, l0_gmlp_ln_g, l0_gmlp_ln_b, l0_spatial_w, spatial_b_exp,
                rows=GMLP_ROWS)

    dt_bias_p = head_lanes(l0_dt_bias).reshape(1, LANES)
    a_log_p = head_lanes(l0_a_log).reshape(1, LANES)
    d_skip_exp = jnp.repeat(l0_d_skip, SSD_HEAD_DIM).reshape(1, SSD_WIDTH)
    lane_id = jnp.arange(LANES)
    expand = ((lane_id[:, None] % SSD_HEADS
               == (jnp.arange(SSD_WIDTH) // SSD_HEAD_DIM)[None, :])
              & (lane_id[:, None] < SPLIT_TERMS * SSD_HEADS)).astype(BF16)
    y_b, w_out0, w_in1, w_out1 = _ssd(
        uvz, xbc, dt_raw, l0_conv_w, l0_conv_b, dt_bias_p, a_log_p, d_skip_exp,
        l0_ssm_norm_g, expand, [l0_w_out, l1_w_in, l1_w_out], batch=batch, rows=SSD_ROWS)

    x1 = _out_proj2(y_a, y_b, w_out0, x2d, tm=PROJ_TM, tn=OUT_EVEN_TN)

    q_scale = LOG2E * DIFF_HEAD_DIM ** -0.5
    (qkvg,) = _norm_matmul(
        x1, l1_norm_g, w_in1,
        [(BF16, [(DIFF_WIDTH, ("scale", q_scale)), (2 * DIFF_WIDTH, None),
                 (DIFF_WIDTH, "silu")])], tm=PROJ_TM, tn=ODD_PROJ_TN)
    slopes = 2.0 ** (-8.0 * (jnp.arange(DIFF_HEADS, dtype=F32) + 1.0) / DIFF_HEADS)
    o = _diff_attn(qkvg, slopes, l1_lambda_q1, l1_lambda_k1, l1_lambda_q2,
                   l1_lambda_k2, l1_subln_g, batch=batch, tq=ATTN_TQ,
                   hps=ATTN_HEADS_PER_STEP)

    out = _out_proj_norm(o, w_out1, x1, final_norm_g, tm=OUT_ODD_TM)
    return out.reshape(batch, seq, d)
```
